```python
import jax, jax.numpy as jnp
from jax import lax
import numpy as np

D_MODEL = 1024
BATCH = 32
SEQ = 2048
DEPTH = 1

GRID_W = 64
Q_BLOCK = 128
ROPE_THETA = 10000.0
MIX_WIDTH = D_MODEL
HEAD_DIM = 64
ATT_WIDTH = MIX_WIDTH // 2
ATT_HEADS = ATT_WIDTH // HEAD_DIM
ATT_KV_HEADS = 2
KV_WIDTH = ATT_KV_HEADS * HEAD_DIM
RWKV_WIDTH = MIX_WIDTH - ATT_WIDTH
RWKV_HEAD = 64
RWKV_HEADS = RWKV_WIDTH // RWKV_HEAD
DECAY_LORA = 64
ICLR_LORA = 64
GATE_LORA = 128
N_SHIFT = 3 * RWKV_WIDTH + DECAY_LORA + ICLR_LORA + GATE_LORA
N_IN = ATT_WIDTH + 2 * KV_WIDTH + N_SHIFT
DECAY_SCALE = 0.606531
GN_EPS = 64e-5
NORM_EPS = 1e-6
L2_EPS = 1e-12
N_EXPERTS = 16
EC_CAPACITY = 2
EXPERT_FF = 1024

kernel_name = "hybrid_attn_rwkv7_ec_moe_block"


def _rmsnorm(x, g):
    xf = x.astype(jnp.float32)
    y = xf * lax.rsqrt(jnp.mean(xf * xf, axis=-1, keepdims=True) + NORM_EPS)
    return y.astype(x.dtype) * g


def _axial_rope_tables(seq):
    rows = seq // GRID_W
    row = jnp.repeat(jnp.arange(rows, dtype=jnp.float32), GRID_W)
    col = jnp.tile(jnp.arange(GRID_W, dtype=jnp.float32), rows)
    n_pairs_axis = HEAD_DIM // 4
    freqs = ROPE_THETA ** (-jnp.arange(n_pairs_axis, dtype=jnp.float32) / n_pairs_axis)
    ang = jnp.concatenate([row[:, None] * freqs, col[:, None] * freqs], axis=-1)
    return jnp.cos(ang), jnp.sin(ang)


def _apply_rope(x, cos, sin):
    xp = x.reshape(*x.shape[:-1], HEAD_DIM // 2, 2)
    x0, x1 = xp[..., 0], xp[..., 1]
    c = cos[None, :, None, :].astype(x.dtype)
    s = sin[None, :, None, :].astype(x.dtype)
    return jnp.stack([x0 * c - x1 * s, x0 * s + x1 * c], axis=-1).reshape(x.shape)


def _block_attention(q, k, v):
    B, S = q.shape[:2]
    nb = S // Q_BLOCK
    grp = ATT_HEADS // ATT_KV_HEADS
    qb = q.reshape(B, nb, Q_BLOCK, ATT_KV_HEADS, grp, HEAD_DIM).transpose(1, 0, 2, 3, 4, 5)
    scale = HEAD_DIM ** -0.5

    def one_block(qi):
        s = jnp.einsum('bqhgd,bkhd->bhgqk', qi, k, preferred_element_type=jnp.float32) * scale
        p = jax.nn.softmax(s, axis=-1).astype(v.dtype)
        return jnp.einsum('bhgqk,bkhd->bqhgd', p, v)

    o = lax.map(one_block, qb)
    return o.transpose(1, 0, 2, 3, 4, 5).reshape(B, S, ATT_WIDTH)


def _centred_shift(p, mu):
    zero = jnp.zeros_like(p[:, :1])
    prev = jnp.concatenate([zero, p[:, :-1]], axis=1)
    nxt = jnp.concatenate([p[:, 1:], zero], axis=1)
    return p + mu * (0.5 * (prev + nxt) - p)


def _wkv7_scan(r, w, k, v, kk, a, reverse):
    B, T, H, N = r.shape

    def step(S, inp):
        r_t, w_t, k_t, v_t, kk_t, a_t = inp
        sa = jnp.einsum('bhvk,bhk->bhv', S, -kk_t)
        S = (S * w_t[:, :, None, :] + sa[..., None] * (kk_t * a_t)[:, :, None, :]
             + v_t[..., None] * k_t[:, :, None, :])
        return S, jnp.einsum('bhvk,bhk->bhv', S, r_t)

    xs = tuple(jnp.swapaxes(z, 0, 1) for z in (r, w, k, v, kk, a))
    S0 = jnp.zeros((B, H, N, N), jnp.float32)
    _, y = lax.scan(step, S0, xs, reverse=reverse)
    return jnp.swapaxes(y, 0, 1)


def _rwkv7_bidir(p, mu_shift, w0, w_up, a0, a_up, g_up, k_k, k_a, r_k, ln_w, ln_b):
    dt = p.dtype
    p = _centred_shift(p, mu_shift)
    R = RWKV_WIDTH
    r, k, v, dw, da, dg = jnp.split(
        p, [R, 2 * R, 3 * R, 3 * R + DECAY_LORA, 3 * R + DECAY_LORA + ICLR_LORA], axis=-1)
    B, T, _ = r.shape

    def heads(z):
        return z.reshape(B, T, RWKV_HEADS, RWKV_HEAD).astype(jnp.float32)

    kk = heads(k * k_k)
    kk = kk / jnp.maximum(jnp.sqrt(jnp.sum(kk * kk, axis=-1, keepdims=True)), L2_EPS)
    g = jax.nn.sigmoid(dg) @ g_up
    tw = jnp.tanh(dw)
    rh, vh = heads(r), heads(v)

    def direction(d, reverse):
        w = jnp.exp(-DECAY_SCALE * jax.nn.sigmoid(w0[d] + tw @ w_up[d]))
        a = jax.nn.sigmoid(a0[d] + da @ a_up[d])
        kd = heads(k * (1.0 + (a - 1.0) * k_a))
        ah = heads(a)
        y = _wkv7_scan(rh, heads(w), kd, vh, kk, ah, reverse)
        bonus = jnp.sum(rh * kd * r_k, axis=-1, keepdims=True) * vh
        return y, bonus

    y_f, bonus_f = direction(0, False)
    y_b, bonus_b = direction(1, True)
    y = y_f + y_b
    mean = jnp.mean(y, axis=-1, keepdims=True)
    var = jnp.mean(jnp.square(y - mean), axis=-1, keepdims=True)
    y = ((y - mean) * lax.rsqrt(var + GN_EPS)).reshape(B, T, R) * ln_w + ln_b
    y = y + (bonus_f + bonus_b).reshape(B, T, R)
    return (y * g).astype(dt)


def _expert_choice_ffn(h, w_router, w_gate, w_up, w_down):
    B, S, D = h.shape
    cap = EC_CAPACITY * S // N_EXPERTS
    aff = jax.nn.softmax((h @ w_router).astype(jnp.float32), axis=-1)
    vals, idx = lax.top_k(jnp.swapaxes(aff, 1, 2), cap)
    hg = jax.vmap(lambda hb, ib: hb[ib])(h, idx)
    hid = (jax.nn.silu(jnp.einsum('becd,edf->becf', hg, w_gate))
           * jnp.einsum('becd,edf->becf', hg, w_up))
    y = jnp.einsum('becf,efd->becd', hid, w_down) * vals[..., None].astype(h.dtype)
    return jax.vmap(
        lambda yb, ib: jnp.zeros((S, D), h.dtype).at[ib.reshape(-1)].add(yb.reshape(-1, D))
    )(y, idx)


def setup_inputs(seed: int = 0) -> dict:
    key = jax.random.key(seed)
    ks = jax.random.split(key, 28)
    L, D, E, F = DEPTH, D_MODEL, N_EXPERTS, EXPERT_FF

    def nrm(k, shape, s):
        return jax.random.normal(k, shape, jnp.float32) * s

    return {
        "x": nrm(ks[0], (BATCH, SEQ, D), 1.0),
        "c": nrm(ks[1], (BATCH, D), 1.0),
        "w_ada": nrm(ks[2], (L, D, 6 * D), 0.5 * D ** -0.5),
        "b_ada": nrm(ks[3], (L, 6 * D), 0.02),
        "g_mix": 1.0 + nrm(ks[4], (L, D), 0.02),
        "w_in": nrm(ks[5], (L, D, N_IN), D ** -0.5),
        "q_norm": 1.0 + nrm(ks[6], (L, HEAD_DIM), 0.02),
        "k_norm": 1.0 + nrm(ks[7], (L, HEAD_DIM), 0.02),
        "mu_shift": jax.random.uniform(ks[8], (L, N_SHIFT), jnp.float32, 0.0, 1.0),
        "w0": nrm(ks[9], (L, 2, RWKV_WIDTH), 1.0),
        "w_up": nrm(ks[10], (L, 2, DECAY_LORA, RWKV_WIDTH), 0.5 * DECAY_LORA ** -0.5),
        "a0": nrm(ks[11], (L, 2, RWKV_WIDTH), 0.5),
        "a_up": nrm(ks[12], (L, 2, ICLR_LORA, RWKV_WIDTH), 0.5 * ICLR_LORA ** -0.5),
        "g_up": nrm(ks[13], (L, GATE_LORA, RWKV_WIDTH), GATE_LORA ** -0.5),
        "k_k": 0.85 + nrm(ks[14], (L, RWKV_WIDTH), 0.05),
        "k_a": 1.0 + nrm(ks[15], (L, RWKV_WIDTH), 0.05),
        "r_k": nrm(ks[16], (L, RWKV_HEADS, RWKV_HEAD), 0.1),
        "ln_w": 1.0 + nrm(ks[17], (L, RWKV_WIDTH), 0.02),
        "ln_b": nrm(ks[18], (L, RWKV_WIDTH), 0.02),
        "w_out": nrm(ks[19], (L, MIX_WIDTH, D), MIX_WIDTH ** -0.5),
        "g_ffn": 1.0 + nrm(ks[20], (L, D), 0.02),
        "w_router": nrm(ks[21], (L, D, E), D ** -0.5),
        "w_gate": nrm(ks[22], (L, E, D, F), D ** -0.5),
        "w_up_e": nrm(ks[23], (L, E, D, F), D ** -0.5),
        "w_down": nrm(ks[24], (L, E, F, D), F ** -0.5),
    }


def reference(x, c, w_ada, b_ada, g_mix, w_in, q_norm, k_norm, mu_shift, w0, w_up, a0,
              a_up, g_up, k_k, k_a, r_k, ln_w, ln_b, w_out, g_ffn, w_router, w_gate,
              w_up_e, w_down):
    B, S, _ = x.shape
    cos, sin = _axial_rope_tables(S)
    cond = jax.nn.silu(c)
    for l in range(DEPTH):
        mod = (cond @ w_ada[l] + b_ada[l])[:, None, :]
        sh1, sc1, gt1, sh2, sc2, gt2 = jnp.split(mod, 6, axis=-1)

        h = _rmsnorm(x, g_mix[l]) * (1.0 + sc1) + sh1
        p = h @ w_in[l]
        pq, pk, pv, pr = jnp.split(
            p, [ATT_WIDTH, ATT_WIDTH + KV_WIDTH, ATT_WIDTH + 2 * KV_WIDTH], axis=-1)
        q = _apply_rope(_rmsnorm(pq.reshape(B, S, ATT_HEADS, HEAD_DIM), q_norm[l]), cos, sin)
        k = _apply_rope(_rmsnorm(pk.reshape(B, S, ATT_KV_HEADS, HEAD_DIM), k_norm[l]), cos, sin)
        v = pv.reshape(B, S, ATT_KV_HEADS, HEAD_DIM)
        o_att = _block_attention(q, k, v)
        o_rwkv = _rwkv7_bidir(pr, mu_shift[l], w0[l], w_up[l], a0[l], a_up[l], g_up[l],
                              k_k[l], k_a[l], r_k[l], ln_w[l], ln_b[l])
        x = x + gt1 * (jnp.concatenate([o_att, o_rwkv], axis=-1) @ w_out[l])

        h2 = _rmsnorm(x, g_ffn[l]) * (1.0 + sc2) + sh2
        x = x + gt2 * _expert_choice_ffn(h2, w_router[l], w_gate[l], w_up_e[l], w_down[l])
    return x
```

```python
import functools

import jax
import jax.numpy as jnp
import numpy as np
from jax import lax
from jax.experimental import pallas as pl
from jax.experimental.pallas import tpu as pltpu

LANES = 128
SUBLANES = 8
VMEM_LIMIT_BYTES = 56 * 1024 * 1024

GRID_W = 64
ROPE_THETA = 10000.0
HEAD_DIM = 64
ATT_KV_HEADS = 2
RWKV_HEAD = 64
DECAY_LORA = 64
ICLR_LORA = 64
GATE_LORA = 128
DECAY_SCALE = 0.606531
GN_EPS = 64e-5
NORM_EPS = 1e-6
L2_EPS = 1e-12
N_EXPERTS = 16
EC_CAPACITY = 2

F32 = jnp.float32
BF16 = jnp.bfloat16


def _cparams(*sem):
    return pltpu.CompilerParams(dimension_semantics=sem, vmem_limit_bytes=VMEM_LIMIT_BYTES)


def _bdot(a, b):
    return jnp.dot(a.astype(BF16), b.astype(BF16), preferred_element_type=F32)


def _split_dot(a, b_bf16):
    hi = a.astype(BF16)
    lo = (a - hi.astype(F32)).astype(BF16)
    return (jnp.dot(hi, b_bf16, preferred_element_type=F32)
            + jnp.dot(lo, b_bf16, preferred_element_type=F32))


def _rms_rows(x):
    return x * lax.rsqrt(jnp.mean(x * x, axis=-1, keepdims=True) + NORM_EPS)


def _mod_kernel(c_ref, w_ref, b_ref, o_ref):
    c = c_ref[...]
    cond = c * jax.nn.sigmoid(c)
    o_ref[...] = _bdot(cond, w_ref[...]) + b_ref[...]


def _modulation(c, w_ada, b_ada):
    B, D = c.shape
    n_out = w_ada.shape[1]
    tn = D
    return pl.pallas_call(
        _mod_kernel,
        grid=(n_out // tn,),
        in_specs=[pl.BlockSpec((B, D), lambda j: (0, 0)),
                  pl.BlockSpec((D, tn), lambda j: (0, j)),
                  pl.BlockSpec((1, tn), lambda j: (0, j))],
        out_specs=pl.BlockSpec((B, tn), lambda j: (0, j)),
        out_shape=jax.ShapeDtypeStruct((B, n_out), F32),
        compiler_params=_cparams("arbitrary"),
        name="adaln_mod",
    )(c, w_ada, b_ada.reshape(1, n_out))


def _rope(x, cos, sin_signed):
    n = x.shape[-1]
    lane = lax.broadcasted_iota(jnp.int32, x.shape, 1)
    nxt = pltpu.roll(x, n - 1, axis=1)
    prv = pltpu.roll(x, 1, axis=1)
    swapped = jnp.where(lane % 2 == 0, nxt, prv)
    return x * cos + swapped * sin_signed


def _inproj_kernel(x_ref, xp_ref, xn_ref, sh_ref, sc_ref, gm_ref, w_ref, seg_ref, cos_ref,
                   sin_ref, qg_ref, kg_ref, mu_ref, wlo_ref, b0_ref, gup_ref,
                   q_out, k_out, v_out, r_out, kr_out, vr_out, wf_out, wb_out, af_out,
                   ab_out, g_out, *, att_w, kv_w, rw):
    i = pl.program_id(1)
    n_i = pl.num_programs(1)
    ts = x_ref.shape[1]
    gain = gm_ref[...]
    scale = 1.0 + sc_ref[0]
    shift = sh_ref[0]

    def norm_mod(xv):
        return (_rms_rows(xv) * gain * scale + shift).astype(BF16)

    h = norm_mod(x_ref[0])
    p = jnp.dot(h, w_ref[...], preferred_element_type=F32)

    seg = seg_ref[...]

    def qk_norm_rope(z, gain_row, reps):
        ms = _split_dot(z * z, seg[: z.shape[1], : z.shape[1]])
        zn = z * lax.rsqrt(ms + NORM_EPS) * gain_row
        cos = jnp.concatenate([cos_ref[...]] * reps, axis=1) if reps > 1 else cos_ref[...]
        sin = jnp.concatenate([sin_ref[...]] * reps, axis=1) if reps > 1 else sin_ref[...]
        return _rope(zn, cos, sin)

    q = qk_norm_rope(p[:, :att_w], qg_ref[...], att_w // LANES)
    q_out[0] = (q * (HEAD_DIM ** -0.5)).astype(BF16)
    k = qk_norm_rope(p[:, att_w:att_w + kv_w], kg_ref[...], kv_w // LANES)
    k_out[0] = k.astype(BF16)
    v_out[0] = p[:, att_w + kv_w:att_w + 2 * kv_w].astype(BF16)

    off = att_w + 2 * kv_w
    pr = p[:, off:]
    halo = jnp.concatenate([xp_ref[0], xn_ref[0]], axis=0)
    ph = jnp.dot(norm_mod(halo), w_ref[:, off:], preferred_element_type=F32)
    prev_row = jnp.where(i == 0, 0.0, ph[SUBLANES - 1:SUBLANES, :])
    next_row = jnp.where(i == n_i - 1, 0.0, ph[SUBLANES:SUBLANES + 1, :])
    row = lax.broadcasted_iota(jnp.int32, pr.shape, 0)
    prev = jnp.where(row == 0, prev_row, pltpu.roll(pr, 1, axis=0))
    nxt = jnp.where(row == ts - 1, next_row, pltpu.roll(pr, ts - 1, axis=0))
    ps = pr + mu_ref[...] * (0.5 * (prev + nxt) - pr)

    r_out[0] = ps[:, :rw]
    kr_out[0] = ps[:, rw:2 * rw]
    vr_out[0] = ps[:, 2 * rw:3 * rw]
    lo_in = ps[:, 3 * rw:3 * rw + DECAY_LORA + ICLR_LORA]
    lane = lax.broadcasted_iota(jnp.int32, lo_in.shape, 1)
    lo_in = jnp.where(lane < DECAY_LORA, jnp.tanh(lo_in), lo_in)
    lo = _bdot(lo_in, wlo_ref[...]) + b0_ref[...]
    wf_out[0] = jnp.exp(-DECAY_SCALE * jax.nn.sigmoid(lo[:, :rw]))
    wb_out[0] = jnp.exp(-DECAY_SCALE * jax.nn.sigmoid(lo[:, rw:2 * rw]))
    af_out[0] = jax.nn.sigmoid(lo[:, 2 * rw:3 * rw])
    ab_out[0] = jax.nn.sigmoid(lo[:, 3 * rw:])
    dg = ps[:, 3 * rw + DECAY_LORA + ICLR_LORA:]
    g_out[0] = _bdot(jax.nn.sigmoid(dg), gup_ref[...])


def _in_projection(x, sh1, sc1, g_mix, w_in, seg, cos_t, sin_t, q_norm, k_norm, mu_shift,
                   wlo, b0, g_up, *, ts):
    B, S, D = x.shape
    n_in = w_in.shape[1]
    rw = g_up.shape[1]
    kv_w = ATT_KV_HEADS * HEAD_DIM
    att_w = n_in - 2 * kv_w - (3 * rw + DECAY_LORA + ICLR_LORA + GATE_LORA)
    n_t = S // ts
    hb = ts // SUBLANES
    n_hb = S // SUBLANES
    const = lambda b, i: (0, 0)
    tile = lambda b, i: (b, i, 0)
    modspec = pl.BlockSpec((1, 1, D), lambda b, i: (b, 0, 0))
    in_specs = [
        pl.BlockSpec((1, ts, D), tile),
        pl.BlockSpec((1, SUBLANES, D), lambda b, i: (b, jnp.maximum(i * hb - 1, 0), 0)),
        pl.BlockSpec((1, SUBLANES, D), lambda b, i: (b, jnp.minimum((i + 1) * hb, n_hb - 1), 0)),
        modspec, modspec,
        pl.BlockSpec((1, D), const),
        pl.BlockSpec((D, n_in), const),
        pl.BlockSpec(seg.shape, const),
        pl.BlockSpec((ts, LANES), lambda b, i: (i, 0)),
        pl.BlockSpec((ts, LANES), lambda b, i: (i, 0)),
        pl.BlockSpec((1, att_w), const),
        pl.BlockSpec((1, kv_w), const),
        pl.BlockSpec((1, mu_shift.shape[1]), const),
        pl.BlockSpec(wlo.shape, const),
        pl.BlockSpec(b0.shape, const),
        pl.BlockSpec(g_up.shape, const),
    ]
    out_shapes = ([jax.ShapeDtypeStruct((B, S, att_w), BF16),
                   jax.ShapeDtypeStruct((B, S, kv_w), BF16),
                   jax.ShapeDtypeStruct((B, S, kv_w), BF16)]
                  + [jax.ShapeDtypeStruct((B, S, rw), F32)] * 8)
    out_specs = ([pl.BlockSpec((1, ts, att_w), tile),
                  pl.BlockSpec((1, ts, kv_w), tile),
                  pl.BlockSpec((1, ts, kv_w), tile)]
                 + [pl.BlockSpec((1, ts, rw), tile)] * 8)
    return pl.pallas_call(
        functools.partial(_inproj_kernel, att_w=att_w, kv_w=kv_w, rw=rw),
        grid=(B, n_t),
        in_specs=in_specs,
        out_specs=out_specs,
        out_shape=out_shapes,
        compiler_params=_cparams("parallel", "arbitrary"),
        name="in_projection",
    )(x, x, x, sh1, sc1, g_mix, w_in, seg, cos_t, sin_t, q_norm, k_norm, mu_shift, wlo, b0, g_up)


def _attn_kernel(q_ref, k_ref, v_ref, o_ref):
    k = k_ref[0, 0]
    v = v_ref[0, 0]
    grp = q_ref.shape[2] // HEAD_DIM
    outs = []
    for hh in range(grp):
        q = q_ref[0, :, hh * HEAD_DIM:(hh + 1) * HEAD_DIM]
        s = lax.dot_general(q, k, (((1,), (1,)), ((), ())), preferred_element_type=F32)
        m = jnp.max(s, axis=-1, keepdims=True)
        e = jnp.exp(s - m)
        l = jnp.sum(e, axis=-1, keepdims=True)
        o = jnp.dot(e.astype(BF16), v, preferred_element_type=F32)
        outs.append(o / l)
    o_ref[0] = jnp.concatenate(outs, axis=1).astype(o_ref.dtype)


def _attention(q, k, v, *, tq):
    B, S, att_w = q.shape
    gw = att_w // ATT_KV_HEADS
    k4 = k.reshape(B, S, ATT_KV_HEADS, HEAD_DIM).transpose(0, 2, 1, 3)
    v4 = v.reshape(B, S, ATT_KV_HEADS, HEAD_DIM).transpose(0, 2, 1, 3)
    return pl.pallas_call(
        _attn_kernel,
        grid=(B, ATT_KV_HEADS, S // tq),
        in_specs=[pl.BlockSpec((1, tq, gw), lambda b, g, i: (b, i, g)),
                  pl.BlockSpec((1, 1, S, HEAD_DIM), lambda b, g, i: (b, g, 0, 0)),
                  pl.BlockSpec((1, 1, S, HEAD_DIM), lambda b, g, i: (b, g, 0, 0))],
        out_specs=pl.BlockSpec((1, tq, gw), lambda b, g, i: (b, i, g)),
        out_shape=jax.ShapeDtypeStruct((B, S, att_w), BF16),
        compiler_params=_cparams("parallel", "parallel", "arbitrary"),
        name="attention",
    )(q, k4, v4)


def _wkv_kernel(r_ref, k_ref, v_ref, w_ref, a_ref, kk_scale_ref, ka_ref, rk_ref,
                y_ref, bs_ref, s_ref, nkk_ref, kka_ref, kd_ref, *, reverse):
    c = pl.program_id(1)
    tc, n, _ = r_ref.shape

    @pl.when(c == 0)
    def _():
        s_ref[...] = jnp.zeros_like(s_ref)

    kt = k_ref[...]
    a = a_ref[...]
    kk = kt * kk_scale_ref[...]
    kk = kk / jnp.maximum(jnp.sqrt(jnp.sum(kk * kk, axis=1, keepdims=True)), L2_EPS)
    nkk_ref[...] = -kk
    kka_ref[...] = kk * a
    kd = kt * (1.0 + (a - 1.0) * ka_ref[...])
    kd_ref[...] = kd
    bs_ref[...] = jnp.sum(r_ref[...] * kd * rk_ref[...], axis=1, keepdims=True)

    def tidx(j):
        return (tc - 1 - j) if reverse else j

    t0 = tidx(0)
    sa0 = jnp.zeros((n, LANES), F32)
    for kx in range(n):
        sa0 = sa0 + s_ref[kx] * nkk_ref[t0, kx:kx + 1, :]

    def step(j, sa):
        t = tidx(j)
        tn = tidx(jnp.minimum(j + 1, tc - 1))
        v_t = v_ref[t]
        y = jnp.zeros((n, LANES), F32)
        sa_next = jnp.zeros((n, LANES), F32)
        for kx in range(n):
            row = pl.ds(kx, 1)
            s_new = (s_ref[kx] * w_ref[t, row, :] + sa * kka_ref[t, row, :]
                     + v_t * kd_ref[t, row, :])
            s_ref[kx] = s_new
            y = y + s_new * r_ref[t, row, :]
            sa_next = sa_next + s_new * nkk_ref[tn, row, :]
        y_ref[t] = y
        return sa_next

    lax.fori_loop(0, tc, step, sa0)


def _wkv_scan(r, k, v, w, a, kk_scale, ka, rk, *, reverse, tc):
    S, N, BH = r.shape
    n_c = S // tc
    n_l = BH // LANES
    if reverse:
        tmap = lambda j, c: (n_c - 1 - c, 0, j)
    else:
        tmap = lambda j, c: (c, 0, j)
    blk = pl.BlockSpec((tc, N, LANES), tmap)
    par = pl.BlockSpec((N, LANES), lambda j, c: (0, j))
    return pl.pallas_call(
        functools.partial(_wkv_kernel, reverse=reverse),
        grid=(n_l, n_c),
        in_specs=[blk] * 5 + [par] * 3,
        out_specs=[blk, pl.BlockSpec((tc, 1, LANES), tmap)],
        out_shape=[jax.ShapeDtypeStruct((S, N, BH), F32),
                   jax.ShapeDtypeStruct((S, 1, BH), F32)],
        scratch_shapes=[pltpu.VMEM((N, N, LANES), F32)] + [pltpu.VMEM((tc, N, LANES), F32)] * 3,
        compiler_params=_cparams("parallel", "arbitrary"),
        name="wkv7_scan_bwd" if reverse else "wkv7_scan_fwd",
    )(r, k, v, w, a, kk_scale, ka, rk)


def _gn_kernel(yf_ref, yb_ref, bf_ref, bb_ref, v_ref, lw_ref, lb_ref, o_ref):
    y = yf_ref[...] + yb_ref[...]
    mean = jnp.mean(y, axis=1, keepdims=True)
    d = y - mean
    var = jnp.mean(d * d, axis=1, keepdims=True)
    o_ref[...] = (d * lax.rsqrt(var + GN_EPS) * lw_ref[...] + lb_ref[...]
                  + (bf_ref[...] + bb_ref[...]) * v_ref[...])


def _group_norm_bonus(yf, yb, bf, bb, v, lw, lb, *, tc):
    S, N, BH = yf.shape
    blk = pl.BlockSpec((tc, N, BH), lambda c: (c, 0, 0))
    sblk = pl.BlockSpec((tc, 1, BH), lambda c: (c, 0, 0))
    par = pl.BlockSpec((N, BH), lambda c: (0, 0))
    return pl.pallas_call(
        _gn_kernel,
        grid=(S // tc,),
        in_specs=[blk, blk, sblk, sblk, blk, par, par],
        out_specs=blk,
        out_shape=jax.ShapeDtypeStruct((S, N, BH), F32),
        compiler_params=_cparams("parallel"),
        name="wkv7_groupnorm",
    )(yf, yb, bf, bb, v, lw, lb)


def _outproj_kernel(oa_ref, yo_ref, g_ref, x_ref, gt_ref, wo_ref, gf_ref, sh_ref, sc_ref,
                    wr_ref, x1_out, h2_out, aff_out):
    mix = jnp.concatenate([oa_ref[0], (yo_ref[0] * g_ref[0]).astype(BF16)], axis=1)
    x1 = x_ref[0] + gt_ref[0] * jnp.dot(mix, wo_ref[...], preferred_element_type=F32)
    x1_out[0] = x1
    h2 = _rms_rows(x1) * gf_ref[...] * (1.0 + sc_ref[0]) + sh_ref[0]
    h2_out[0] = h2.astype(BF16)
    wr = wr_ref[...]
    nt = (((1,), (1,)), ((), ()))
    w_hi = wr.astype(BF16)
    w_lo = (wr - w_hi.astype(F32)).astype(BF16)
    h_hi = h2.astype(BF16)
    h_lo = (h2 - h_hi.astype(F32)).astype(BF16)
    logits = (lax.dot_general(w_hi, h_hi, nt, preferred_element_type=F32)
              + lax.dot_general(w_hi, h_lo, nt, preferred_element_type=F32)
              + lax.dot_general(w_lo, h_hi, nt, preferred_element_type=F32))
    m = jnp.max(logits, axis=0, keepdims=True)
    e = jnp.exp(logits - m)
    aff_out[0] = e / jnp.sum(e, axis=0, keepdims=True)


def _out_projection(o_att, yo, g, x, gt1, w_out, g_ffn, sh2, sc2, w_router_t, *, ts):
    B, S, D = x.shape
    aw = o_att.shape[2]
    rw = yo.shape[2]
    E = w_router_t.shape[0]
    tile = lambda b, i: (b, i, 0)
    const = lambda b, i: (0, 0)
    modspec = pl.BlockSpec((1, 1, D), lambda b, i: (b, 0, 0))
    return pl.pallas_call(
        _outproj_kernel,
        grid=(B, S // ts),
        in_specs=[pl.BlockSpec((1, ts, aw), tile),
                  pl.BlockSpec((1, ts, rw), tile),
                  pl.BlockSpec((1, ts, rw), tile),
                  pl.BlockSpec((1, ts, D), tile),
                  modspec,
                  pl.BlockSpec((aw + rw, D), const),
                  pl.BlockSpec((1, D), const),
                  modspec, modspec,
                  pl.BlockSpec((E, D), const)],
        out_specs=[pl.BlockSpec((1, ts, D), tile),
                   pl.BlockSpec((1, ts, D), tile),
                   pl.BlockSpec((1, E, ts), lambda b, i: (b, 0, i))],
        out_shape=[jax.ShapeDtypeStruct((B, S, D), F32),
                   jax.ShapeDtypeStruct((B, S, D), BF16),
                   jax.ShapeDtypeStruct((B, E, S), F32)],
        compiler_params=_cparams("parallel", "arbitrary"),
        name="out_projection",
    )(o_att, yo, g, x, gt1, w_out, g_ffn, sh2, sc2, w_router_t)


def _route_kernel(aff_ref, tri_ref, wgt_out, pos_out, *, cap):
    aff = aff_ref[0]
    bits = pltpu.bitcast(aff, jnp.int32)
    capf = jnp.float32(cap)

    def count(mask):
        return jnp.sum(jnp.where(mask, 1.0, 0.0), axis=1, keepdims=True)

    def bit_step(i, thr):
        cand = thr | jnp.left_shift(jnp.int32(1), 30 - i)
        return jnp.where(count(bits >= cand) >= capf, cand, thr)

    thr = lax.fori_loop(0, 31, bit_step, jnp.zeros((aff.shape[0], 1), jnp.int32))
    gt = bits > thr
    eq = bits == thr
    need = capf - count(gt)
    tri = tri_ref[...]
    eq_rank = jnp.dot(jnp.where(eq, 1.0, 0.0).astype(BF16), tri, preferred_element_type=F32)
    sel = jnp.where(gt, 1.0, jnp.where(eq, jnp.where(eq_rank <= need, 1.0, 0.0), 0.0))
    rank = jnp.dot(sel.astype(BF16), tri, preferred_element_type=F32)
    chosen = sel > 0.5
    wgt_out[0] = jnp.where(chosen, aff, 0.0)
    pos_out[0] = jnp.where(chosen, rank - 1.0, -1.0).astype(jnp.int32)


def _routing(aff, tri, *, cap):
    B, E, S = aff.shape
    blk = pl.BlockSpec((1, E, S), lambda b: (b, 0, 0))
    return pl.pallas_call(
        functools.partial(_route_kernel, cap=cap),
        grid=(B,),
        in_specs=[blk, pl.BlockSpec((S, S), lambda b: (0, 0))],
        out_specs=[blk, blk],
        out_shape=[jax.ShapeDtypeStruct((B, E, S), F32),
                   jax.ShapeDtypeStruct((B, E, S), jnp.int32)],
        compiler_params=_cparams("parallel"),
        name="ec_routing",
    )(aff, tri)


def _expert_kernel(h_ref, pos_ref, wgt_ref, wg_ref, wu_ref, wd_ref, acc_ref, *, cap):
    e = pl.program_id(1)

    @pl.when(e == 0)
    def _():
        acc_ref[...] = jnp.zeros_like(acc_ref)

    pos = pos_ref[0, 0]
    wgt = wgt_ref[0, 0]
    s = pos.shape[1]
    slot = lax.broadcasted_iota(jnp.int32, (cap, s), 0)
    hit = pos == slot
    onehot = jnp.where(hit, 1.0, 0.0).astype(BF16)
    vals = jnp.sum(jnp.where(hit, wgt, 0.0), axis=1, keepdims=True)
    hg = jnp.dot(onehot, h_ref[0], preferred_element_type=F32).astype(BF16)
    gate = jnp.dot(hg, wg_ref[0], preferred_element_type=F32)
    up = jnp.dot(hg, wu_ref[0], preferred_element_type=F32)
    hid = (gate * jax.nn.sigmoid(gate) * up).astype(BF16)
    y = jnp.dot(hid, wd_ref[0], preferred_element_type=F32) * vals
    acc_ref[0] += lax.dot_general(onehot, y.astype(BF16), (((0,), (0,)), ((), ())),
                                  preferred_element_type=F32)


def _experts(h2, pos, wgt, w_gate, w_up, w_down, *, cap):
    B, S, D = h2.shape
    E, _, F = w_gate.shape
    row = pl.BlockSpec((1, 1, 1, S), lambda b, e: (b, e, 0, 0))
    return pl.pallas_call(
        functools.partial(_expert_kernel, cap=cap),
        grid=(B, E),
        in_specs=[pl.BlockSpec((1, S, D), lambda b, e: (b, 0, 0)),
                  row, row,
                  pl.BlockSpec((1, D, F), lambda b, e: (e, 0, 0)),
                  pl.BlockSpec((1, D, F), lambda b, e: (e, 0, 0)),
                  pl.BlockSpec((1, F, D), lambda b, e: (e, 0, 0))],
        out_specs=pl.BlockSpec((1, S, D), lambda b, e: (b, 0, 0)),
        out_shape=jax.ShapeDtypeStruct((B, S, D), F32),
        compiler_params=_cparams("parallel", "arbitrary"),
        name="ec_experts",
    )(h2, pos.reshape(B, E, 1, S), wgt.reshape(B, E, 1, S), w_gate, w_up, w_down)


def _residual_kernel(x_ref, a_ref, gt_ref, o_ref):
    o_ref[0] = x_ref[0] + gt_ref[0] * a_ref[0]


def _residual(x1, acc, gt2, *, ts):
    B, S, D = x1.shape
    tile = pl.BlockSpec((1, ts, D), lambda b, i: (b, i, 0))
    return pl.pallas_call(
        _residual_kernel,
        grid=(B, S // ts),
        in_specs=[tile, tile, pl.BlockSpec((1, 1, D), lambda b, i: (b, 0, 0))],
        out_specs=tile,
        out_shape=jax.ShapeDtypeStruct((B, S, D), F32),
        compiler_params=_cparams("parallel", "parallel"),
        name="ffn_residual",
    )(x1, acc, gt2)


def _rope_tables(seq):
    rows = seq // GRID_W
    row = jnp.repeat(jnp.arange(rows, dtype=F32), GRID_W)
    col = jnp.tile(jnp.arange(GRID_W, dtype=F32), rows)
    n_pairs_axis = HEAD_DIM // 4
    freqs = ROPE_THETA ** (-jnp.arange(n_pairs_axis, dtype=F32) / n_pairs_axis)
    ang = jnp.concatenate([row[:, None] * freqs, col[:, None] * freqs], axis=-1)
    cos = jnp.repeat(jnp.cos(ang), 2, axis=1)
    sin = jnp.repeat(jnp.sin(ang), 2, axis=1) * jnp.tile(jnp.array([-1.0, 1.0], F32), HEAD_DIM // 2)
    reps = LANES // HEAD_DIM
    return jnp.tile(cos, (1, reps)), jnp.tile(sin, (1, reps))


def _to_scan_layout(z, bh_pad):
    B, S, R = z.shape
    H = R // RWKV_HEAD
    zt = z.reshape(B, S, H, RWKV_HEAD).transpose(1, 3, 0, 2).reshape(S, RWKV_HEAD, B * H)
    if bh_pad != B * H:
        zt = jnp.pad(zt, ((0, 0), (0, 0), (0, bh_pad - B * H)))
    return zt


def _from_scan_layout(zt, B):
    S, N, _ = zt.shape
    H = zt.shape[2] // B if zt.shape[2] % B == 0 else None
    return zt, H


def _lane_param(p_hn, B, bh_pad):
    H, N = p_hn.shape
    t = jnp.tile(p_hn.T[:, None, :], (1, B, 1)).reshape(N, B * H)
    if bh_pad != B * H:
        t = jnp.pad(t, ((0, 0), (0, bh_pad - B * H)))
    return t


def _pick_tile(n, pref):
    t = min(n, pref)
    while n % t:
        t //= 2
    return t


def kernel(x, c, w_ada, b_ada, g_mix, w_in, q_norm, k_norm, mu_shift, w0, w_up, a0, a_up, g_up,
           k_k, k_a, r_k, ln_w, ln_b, w_out, g_ffn, w_router, w_gate, w_up_e, w_down):
    B, S, D = x.shape
    depth = w_ada.shape[0]
    rw = g_up.shape[2]
    H = rw // RWKV_HEAD
    cap = EC_CAPACITY * S // N_EXPERTS
    ts = _pick_tile(S, 256)
    tq = _pick_tile(S, 512)
    tc = _pick_tile(S, 64)
    bh = B * H
    bh_pad = -(-bh // LANES) * LANES

    cos_t, sin_t = _rope_tables(S)
    seg_n = max(w_in.shape[2] - 2 * ATT_KV_HEADS * HEAD_DIM
                - (3 * rw + DECAY_LORA + ICLR_LORA + GATE_LORA), ATT_KV_HEADS * HEAD_DIM)
    seg_id = np.arange(seg_n) // HEAD_DIM
    seg = jnp.asarray((seg_id[:, None] == seg_id[None, :]).astype(np.float32) / HEAD_DIM, BF16)
    tri = jnp.asarray(np.arange(S)[:, None] <= np.arange(S)[None, :], BF16)

    for l in range(depth):
        mod = _modulation(c, w_ada[l].astype(BF16), b_ada[l])
        sh1, sc1, gt1, sh2, sc2, gt2 = [m.reshape(B, 1, D) for m in jnp.split(mod, 6, axis=-1)]

        zeros = jnp.zeros((DECAY_LORA, rw), F32)
        wlo = jnp.concatenate([
            jnp.concatenate([w_up[l, 0], w_up[l, 1], zeros, zeros], axis=1),
            jnp.concatenate([zeros, zeros, a_up[l, 0], a_up[l, 1]], axis=1)], axis=0).astype(BF16)
        b0 = jnp.concatenate([w0[l, 0], w0[l, 1], a0[l, 0], a0[l, 1]]).reshape(1, 4 * rw)
        att_w = seg_n
        qg = jnp.tile(q_norm[l], att_w // HEAD_DIM).reshape(1, att_w)
        kg = jnp.tile(k_norm[l], ATT_KV_HEADS).reshape(1, ATT_KV_HEADS * HEAD_DIM)

        (q, k, v, r, kr, vr, wf, wb, af, ab, g) = _in_projection(
            x, sh1, sc1, g_mix[l].reshape(1, D), w_in[l].astype(BF16), seg, cos_t, sin_t, qg, kg,
            mu_shift[l].reshape(1, -1), wlo, b0, g_up[l].astype(BF16), ts=ts)

        o_att = _attention(q, k, v, tq=tq)

        rt, kt, vt, wft, wbt, aft, abt = [_to_scan_layout(z, bh_pad)
                                          for z in (r, kr, vr, wf, wb, af, ab)]
        kk_scale = _lane_param(k_k[l].reshape(H, RWKV_HEAD), B, bh_pad)
        ka = _lane_param(k_a[l].reshape(H, RWKV_HEAD), B, bh_pad)
        rk = _lane_param(r_k[l], B, bh_pad)
        yf, bf = _wkv_scan(rt, kt, vt, wft, aft, kk_scale, ka, rk, reverse=False, tc=tc)
        yb, bb = _wkv_scan(rt, kt, vt, wbt, abt, kk_scale, ka, rk, reverse=True, tc=tc)
        lw = _lane_param(ln_w[l].reshape(H, RWKV_HEAD), B, bh_pad)
        lb = _lane_param(ln_b[l].reshape(H, RWKV_HEAD), B, bh_pad)
        yo_t = _group_norm_bonus(yf, yb, bf, bb, vt, lw, lb, tc=tc)
        yo = (yo_t[:, :, :bh].reshape(S, RWKV_HEAD, B, H).transpose(2, 0, 3, 1)
              .reshape(B, S, rw))

        x1, h2, aff = _out_projection(o_att, yo, g, x, gt1, w_out[l].astype(BF16),
                                      g_ffn[l].reshape(1, D), sh2, sc2, w_router[l].T, ts=ts)
        wgt, pos = _routing(aff, tri, cap=cap)
        acc = _experts(h2, pos, wgt, w_gate[l].astype(BF16), w_up_e[l].astype(BF16),
                       w_down[l].astype(BF16), cap=cap)
        x = _residual(x1, acc, gt2, ts=ts)
    return x
```

```python
import functools

import jax
import jax.numpy as jnp
import numpy as np
from jax import lax
from jax.experimental import pallas as pl
from jax.experimental.pallas import tpu as pltpu

LANES = 128
SUBLANES = 8
VMEM_LIMIT_BYTES = 56 * 1024 * 1024

GRID_W = 64
ROPE_THETA = 10000.0
HEAD_DIM = 64
ATT_KV_HEADS = 2
RWKV_HEAD = 64
DECAY_LORA = 64
ICLR_LORA = 64
GATE_LORA = 128
DECAY_SCALE = 0.606531
GN_EPS = 64e-5
NORM_EPS = 1e-6
L2_EPS = 1e-12
N_EXPERTS = 16
EC_CAPACITY = 2
LOG2_E = 1.4426950408889634

F32 = jnp.float32
BF16 = jnp.bfloat16


def _cparams(*sem):
    return pltpu.CompilerParams(dimension_semantics=sem, vmem_limit_bytes=VMEM_LIMIT_BYTES)


def _bdot(a, b):
    return jnp.dot(a.astype(BF16), b.astype(BF16), preferred_element_type=F32)


def _split_dot(a, b_bf16):
    hi = a.astype(BF16)
    lo = (a - hi.astype(F32)).astype(BF16)
    return (jnp.dot(hi, b_bf16, preferred_element_type=F32)
            + jnp.dot(lo, b_bf16, preferred_element_type=F32))


def _rms_rows(x):
    return x * lax.rsqrt(jnp.mean(x * x, axis=-1, keepdims=True) + NORM_EPS)


def _mod_kernel(c_ref, w_ref, b_ref, o_ref):
    c = c_ref[...]
    cond = c * jax.nn.sigmoid(c)
    o_ref[...] = _bdot(cond, w_ref[...]) + b_ref[...]


def _modulation(c, w_ada, b_ada):
    B, D = c.shape
    n_out = w_ada.shape[1]
    tn = D
    return pl.pallas_call(
        _mod_kernel,
        grid=(n_out // tn,),
        in_specs=[pl.BlockSpec((B, D), lambda j: (0, 0)),
                  pl.BlockSpec((D, tn), lambda j: (0, j)),
                  pl.BlockSpec((1, tn), lambda j: (0, j))],
        out_specs=pl.BlockSpec((B, tn), lambda j: (0, j)),
        out_shape=jax.ShapeDtypeStruct((B, n_out), F32),
        compiler_params=_cparams("arbitrary"),
        name="adaln_mod",
    )(c, w_ada, b_ada.reshape(1, n_out))


def _rope(x, cos, sin_signed):
    n = x.shape[-1]
    lane = lax.broadcasted_iota(jnp.int32, x.shape, 1)
    nxt = pltpu.roll(x, n - 1, axis=1)
    prv = pltpu.roll(x, 1, axis=1)
    swapped = jnp.where(lane % 2 == 0, nxt, prv)
    return x * cos + swapped * sin_signed


def _inproj_kernel(x_ref, xp_ref, xn_ref, sh_ref, sc_ref, gm_ref, w_ref, seg_ref, cos_ref,
                   sin_ref, qg_ref, kg_ref, mu_ref, wlo_ref, b0_ref, gup_ref,
                   q_out, k_out, v_out, r_out, kr_out, vr_out, wf_out, wb_out, af_out,
                   ab_out, g_out, *, att_w, kv_w, rw):
    i = pl.program_id(1)
    n_i = pl.num_programs(1)
    ts = x_ref.shape[1]
    gain = gm_ref[...]
    scale = 1.0 + sc_ref[0]
    shift = sh_ref[0]

    def norm_mod(xv):
        return (_rms_rows(xv) * gain * scale + shift).astype(BF16)

    h = norm_mod(x_ref[0])
    p = jnp.dot(h, w_ref[...], preferred_element_type=F32)

    seg = seg_ref[...]

    def qk_norm_rope(z, gain_row, reps):
        ms = _split_dot(z * z, seg[: z.shape[1], : z.shape[1]])
        zn = z * lax.rsqrt(ms + NORM_EPS) * gain_row
        cos = jnp.concatenate([cos_ref[...]] * reps, axis=1) if reps > 1 else cos_ref[...]
        sin = jnp.concatenate([sin_ref[...]] * reps, axis=1) if reps > 1 else sin_ref[...]
        return _rope(zn, cos, sin)

    q = qk_norm_rope(p[:, :att_w], qg_ref[...], att_w // LANES)
    q_out[0] = (q * (HEAD_DIM ** -0.5 * LOG2_E)).astype(BF16)
    k = qk_norm_rope(p[:, att_w:att_w + kv_w], kg_ref[...], kv_w // LANES)
    k_out[0] = k.astype(BF16)
    v_out[0] = p[:, att_w + kv_w:att_w + 2 * kv_w].astype(BF16)

    off = att_w + 2 * kv_w
    pr = p[:, off:]
    halo = jnp.concatenate([xp_ref[0], xn_ref[0]], axis=0)
    ph = jnp.dot(norm_mod(halo), w_ref[:, off:], preferred_element_type=F32)
    prev_row = jnp.where(i == 0, 0.0, ph[SUBLANES - 1:SUBLANES, :])
    next_row = jnp.where(i == n_i - 1, 0.0, ph[SUBLANES:SUBLANES + 1, :])
    row = lax.broadcasted_iota(jnp.int32, pr.shape, 0)
    prev = jnp.where(row == 0, prev_row, pltpu.roll(pr, 1, axis=0))
    nxt = jnp.where(row == ts - 1, next_row, pltpu.roll(pr, ts - 1, axis=0))
    ps = pr + mu_ref[...] * (0.5 * (prev + nxt) - pr)

    r_out[0] = ps[:, :rw]
    kr_out[0] = ps[:, rw:2 * rw]
    vr_out[0] = ps[:, 2 * rw:3 * rw]
    lo_in = ps[:, 3 * rw:3 * rw + DECAY_LORA + ICLR_LORA]
    lane = lax.broadcasted_iota(jnp.int32, lo_in.shape, 1)
    lo_in = jnp.where(lane < DECAY_LORA, jnp.tanh(lo_in), lo_in)
    lo = _bdot(lo_in, wlo_ref[...]) + b0_ref[...]
    wf_out[0] = jnp.exp(-DECAY_SCALE * jax.nn.sigmoid(lo[:, :rw]))
    wb_out[0] = jnp.exp(-DECAY_SCALE * jax.nn.sigmoid(lo[:, rw:2 * rw]))
    af_out[0] = jax.nn.sigmoid(lo[:, 2 * rw:3 * rw])
    ab_out[0] = jax.nn.sigmoid(lo[:, 3 * rw:])
    dg = ps[:, 3 * rw + DECAY_LORA + ICLR_LORA:]
    g_out[0] = _bdot(jax.nn.sigmoid(dg), gup_ref[...])


def _in_projection(x, sh1, sc1, g_mix, w_in, seg, cos_t, sin_t, q_norm, k_norm, mu_shift,
                   wlo, b0, g_up, *, ts):
    B, S, D = x.shape
    n_in = w_in.shape[1]
    rw = g_up.shape[1]
    kv_w = ATT_KV_HEADS * HEAD_DIM
    att_w = n_in - 2 * kv_w - (3 * rw + DECAY_LORA + ICLR_LORA + GATE_LORA)
    n_t = S // ts
    hb = ts // SUBLANES
    n_hb = S // SUBLANES
    const = lambda b, i: (0, 0)
    tile = lambda b, i: (b, i, 0)
    modspec = pl.BlockSpec((1, 1, D), lambda b, i: (b, 0, 0))
    in_specs = [
        pl.BlockSpec((1, ts, D), tile),
        pl.BlockSpec((1, SUBLANES, D), lambda b, i: (b, jnp.maximum(i * hb - 1, 0), 0)),
        pl.BlockSpec((1, SUBLANES, D), lambda b, i: (b, jnp.minimum((i + 1) * hb, n_hb - 1), 0)),
        modspec, modspec,
        pl.BlockSpec((1, D), const),
        pl.BlockSpec((D, n_in), const),
        pl.BlockSpec(seg.shape, const),
        pl.BlockSpec((ts, LANES), lambda b, i: (i, 0)),
        pl.BlockSpec((ts, LANES), lambda b, i: (i, 0)),
        pl.BlockSpec((1, att_w), const),
        pl.BlockSpec((1, kv_w), const),
        pl.BlockSpec((1, mu_shift.shape[1]), const),
        pl.BlockSpec(wlo.shape, const),
        pl.BlockSpec(b0.shape, const),
        pl.BlockSpec(g_up.shape, const),
    ]
    out_shapes = ([jax.ShapeDtypeStruct((B, S, att_w), BF16),
                   jax.ShapeDtypeStruct((B, S, kv_w), BF16),
                   jax.ShapeDtypeStruct((B, S, kv_w), BF16)]
                  + [jax.ShapeDtypeStruct((B, S, rw), F32)] * 8)
    out_specs = ([pl.BlockSpec((1, ts, att_w), tile),
                  pl.BlockSpec((1, ts, kv_w), tile),
                  pl.BlockSpec((1, ts, kv_w), tile)]
                 + [pl.BlockSpec((1, ts, rw), tile)] * 8)
    return pl.pallas_call(
        functools.partial(_inproj_kernel, att_w=att_w, kv_w=kv_w, rw=rw),
        grid=(B, n_t),
        in_specs=in_specs,
        out_specs=out_specs,
        out_shape=out_shapes,
        compiler_params=_cparams("parallel", "arbitrary"),
        name="in_projection",
    )(x, x, x, sh1, sc1, g_mix, w_in, seg, cos_t, sin_t, q_norm, k_norm, mu_shift, wlo, b0, g_up)


def _attn_kernel(q_ref, k_ref, v_ref, o_ref):
    k = k_ref[0, 0]
    v = v_ref[0, 0]
    grp = q_ref.shape[2] // HEAD_DIM
    outs = []
    for hh in range(grp):
        q = q_ref[0, :, hh * HEAD_DIM:(hh + 1) * HEAD_DIM]
        s = lax.dot_general(q, k, (((1,), (1,)), ((), ())), preferred_element_type=F32)
        m = jnp.max(s, axis=-1, keepdims=True)
        e = jnp.exp2(s - m)
        l = jnp.sum(e, axis=-1, keepdims=True)
        o = jnp.dot(e.astype(BF16), v, preferred_element_type=F32)
        outs.append(o / l)
    o_ref[0] = jnp.concatenate(outs, axis=1).astype(o_ref.dtype)


def _attention(q, k, v, *, tq):
    B, S, att_w = q.shape
    gw = att_w // ATT_KV_HEADS
    k4 = k.reshape(B, S, ATT_KV_HEADS, HEAD_DIM).transpose(0, 2, 1, 3)
    v4 = v.reshape(B, S, ATT_KV_HEADS, HEAD_DIM).transpose(0, 2, 1, 3)
    return pl.pallas_call(
        _attn_kernel,
        grid=(B, ATT_KV_HEADS, S // tq),
        in_specs=[pl.BlockSpec((1, tq, gw), lambda b, g, i: (b, i, g)),
                  pl.BlockSpec((1, 1, S, HEAD_DIM), lambda b, g, i: (b, g, 0, 0)),
                  pl.BlockSpec((1, 1, S, HEAD_DIM), lambda b, g, i: (b, g, 0, 0))],
        out_specs=pl.BlockSpec((1, tq, gw), lambda b, g, i: (b, i, g)),
        out_shape=jax.ShapeDtypeStruct((B, S, att_w), BF16),
        compiler_params=_cparams("parallel", "parallel", "arbitrary"),
        name="attention",
    )(q, k4, v4)


def _wkv_kernel(r_ref, k_ref, v_ref, w_ref, a_ref, kk_scale_ref, ka_ref, rk_ref,
                y_ref, bs_ref, s_ref, g_ref, ca_ref, cb_ref, cr_ref, cn_ref, *, reverse):
    c = pl.program_id(1)
    tc, n, _ = r_ref.shape

    @pl.when(c == 0)
    def _():
        s_ref[...] = jnp.zeros_like(s_ref)

    def tidx(j):
        return (tc - 1 - j) if reverse else j

    def prepare(j, g):
        t = tidx(j)
        kt = k_ref[t]
        a = a_ref[t]
        r = r_ref[t]
        kk = kt * kk_scale_ref[...]
        kk = kk / jnp.maximum(jnp.sqrt(jnp.sum(kk * kk, axis=0, keepdims=True)), L2_EPS)
        kd = kt * (1.0 + (a - 1.0) * ka_ref[...])
        bs_ref[t] = jnp.sum(r * kd * rk_ref[...], axis=0, keepdims=True)
        cn_ref[t] = -(kk * g)
        g = g * w_ref[t]
        g_inv = 1.0 / g
        ca_ref[t] = kk * a * g_inv
        cb_ref[t] = kd * g_inv
        cr_ref[t] = r * g
        return g

    g_ref[...] = lax.fori_loop(0, tc, prepare, jnp.ones((n, LANES), F32), unroll=4)

    t0 = tidx(0)
    sa0 = jnp.zeros((n, LANES), F32)
    for kx in range(n):
        sa0 = sa0 + s_ref[kx] * cn_ref[t0, kx:kx + 1, :]

    def step(j, sa):
        t = tidx(j)
        tn = tidx(jnp.minimum(j + 1, tc - 1))
        v_t = v_ref[t]
        y = jnp.zeros((n, LANES), F32)
        sa_next = jnp.zeros((n, LANES), F32)
        for kx in range(n):
            row = pl.ds(kx, 1)
            z_new = s_ref[kx] + sa * ca_ref[t, row, :] + v_t * cb_ref[t, row, :]
            s_ref[kx] = z_new
            y = y + z_new * cr_ref[t, row, :]
            sa_next = sa_next + z_new * cn_ref[tn, row, :]
        y_ref[t] = y
        return sa_next

    lax.fori_loop(0, tc, step, sa0)

    for kx in range(n):
        s_ref[kx] = s_ref[kx] * g_ref[kx:kx + 1, :]


def _wkv_scan(r, k, v, w, a, kk_scale, ka, rk, *, reverse, tc):
    S, N, BH = r.shape
    n_c = S // tc
    n_l = BH // LANES
    if reverse:
        tmap = lambda j, c: (n_c - 1 - c, 0, j)
    else:
        tmap = lambda j, c: (c, 0, j)
    blk = pl.BlockSpec((tc, N, LANES), tmap)
    par = pl.BlockSpec((N, LANES), lambda j, c: (0, j))
    return pl.pallas_call(
        functools.partial(_wkv_kernel, reverse=reverse),
        grid=(n_l, n_c),
        in_specs=[blk] * 5 + [par] * 3,
        out_specs=[blk, pl.BlockSpec((tc, 1, LANES), tmap)],
        out_shape=[jax.ShapeDtypeStruct((S, N, BH), F32),
                   jax.ShapeDtypeStruct((S, 1, BH), F32)],
        scratch_shapes=([pltpu.VMEM((N, N, LANES), F32), pltpu.VMEM((N, LANES), F32)]
                        + [pltpu.VMEM((tc, N, LANES), F32)] * 4),
        compiler_params=_cparams("parallel", "arbitrary"),
        name="wkv7_scan_bwd" if reverse else "wkv7_scan_fwd",
    )(r, k, v, w, a, kk_scale, ka, rk)


def _gn_kernel(yf_ref, yb_ref, bf_ref, bb_ref, v_ref, lw_ref, lb_ref, o_ref):
    y = yf_ref[...] + yb_ref[...]
    mean = jnp.mean(y, axis=1, keepdims=True)
    d = y - mean
    var = jnp.mean(d * d, axis=1, keepdims=True)
    o_ref[...] = (d * lax.rsqrt(var + GN_EPS) * lw_ref[...] + lb_ref[...]
                  + (bf_ref[...] + bb_ref[...]) * v_ref[...])


def _group_norm_bonus(yf, yb, bf, bb, v, lw, lb, *, tc):
    S, N, BH = yf.shape
    blk = pl.BlockSpec((tc, N, BH), lambda c: (c, 0, 0))
    sblk = pl.BlockSpec((tc, 1, BH), lambda c: (c, 0, 0))
    par = pl.BlockSpec((N, BH), lambda c: (0, 0))
    return pl.pallas_call(
        _gn_kernel,
        grid=(S // tc,),
        in_specs=[blk, blk, sblk, sblk, blk, par, par],
        out_specs=blk,
        out_shape=jax.ShapeDtypeStruct((S, N, BH), F32),
        compiler_params=_cparams("parallel"),
        name="wkv7_groupnorm",
    )(yf, yb, bf, bb, v, lw, lb)


def _outproj_kernel(oa_ref, yo_ref, g_ref, x_ref, gt_ref, wo_ref, gf_ref, sh_ref, sc_ref,
                    wr_ref, x1_out, h2_out, aff_out):
    mix = jnp.concatenate([oa_ref[0], (yo_ref[0] * g_ref[0]).astype(BF16)], axis=1)
    x1 = x_ref[0] + gt_ref[0] * jnp.dot(mix, wo_ref[...], preferred_element_type=F32)
    x1_out[0] = x1
    h2 = _rms_rows(x1) * gf_ref[...] * (1.0 + sc_ref[0]) + sh_ref[0]
    h2_out[0] = h2.astype(BF16)
    wr = wr_ref[...]
    nt = (((1,), (1,)), ((), ()))
    w_hi = wr.astype(BF16)
    w_lo = (wr - w_hi.astype(F32)).astype(BF16)
    h_hi = h2.astype(BF16)
    h_lo = (h2 - h_hi.astype(F32)).astype(BF16)
    logits = (lax.dot_general(w_hi, h_hi, nt, preferred_element_type=F32)
              + lax.dot_general(w_hi, h_lo, nt, preferred_element_type=F32)
              + lax.dot_general(w_lo, h_hi, nt, preferred_element_type=F32))
    m = jnp.max(logits, axis=0, keepdims=True)
    e = jnp.exp(logits - m)
    aff_out[0] = e / jnp.sum(e, axis=0, keepdims=True)


def _out_projection(o_att, yo, g, x, gt1, w_out, g_ffn, sh2, sc2, w_router_t, *, ts):
    B, S, D = x.shape
    aw = o_att.shape[2]
    rw = yo.shape[2]
    E = w_router_t.shape[0]
    tile = lambda b, i: (b, i, 0)
    const = lambda b, i: (0, 0)
    modspec = pl.BlockSpec((1, 1, D), lambda b, i: (b, 0, 0))
    return pl.pallas_call(
        _outproj_kernel,
        grid=(B, S // ts),
        in_specs=[pl.BlockSpec((1, ts, aw), tile),
                  pl.BlockSpec((1, ts, rw), tile),
                  pl.BlockSpec((1, ts, rw), tile),
                  pl.BlockSpec((1, ts, D), tile),
                  modspec,
                  pl.BlockSpec((aw + rw, D), const),
                  pl.BlockSpec((1, D), const),
                  modspec, modspec,
                  pl.BlockSpec((E, D), const)],
        out_specs=[pl.BlockSpec((1, ts, D), tile),
                   pl.BlockSpec((1, ts, D), tile),
                   pl.BlockSpec((1, E, ts), lambda b, i: (b, 0, i))],
        out_shape=[jax.ShapeDtypeStruct((B, S, D), F32),
                   jax.ShapeDtypeStruct((B, S, D), BF16),
                   jax.ShapeDtypeStruct((B, E, S), F32)],
        compiler_params=_cparams("parallel", "arbitrary"),
        name="out_projection",
    )(o_att, yo, g, x, gt1, w_out, g_ffn, sh2, sc2, w_router_t)


def _route_kernel(aff_ref, tri_ref, wgt_out, pos_out, *, cap):
    aff = aff_ref[0]
    capf = jnp.float32(cap)

    def count(mask):
        return jnp.sum(jnp.where(mask, 1.0, 0.0), axis=1, keepdims=True)

    def as_float(word):
        return pltpu.bitcast(word, F32)

    def bit_step(i, thr):
        cand = thr | jnp.left_shift(jnp.int32(1), 30 - i)
        return jnp.where(count(aff >= as_float(cand)) >= capf, cand, thr)

    thr = lax.fori_loop(0, 31, bit_step, jnp.zeros((aff.shape[0], 1), jnp.int32))
    thr_f = as_float(thr)
    gt = aff > thr_f
    eq = aff == thr_f
    need = capf - count(gt)
    tri = tri_ref[...]
    eq_rank = jnp.dot(jnp.where(eq, 1.0, 0.0).astype(BF16), tri, preferred_element_type=F32)
    sel = jnp.where(gt, 1.0, jnp.where(eq, jnp.where(eq_rank <= need, 1.0, 0.0), 0.0))
    rank = jnp.dot(sel.astype(BF16), tri, preferred_element_type=F32)
    chosen = sel > 0.5
    wgt_out[0] = jnp.where(chosen, aff, 0.0)
    pos_out[0] = jnp.where(chosen, rank - 1.0, -1.0).astype(jnp.int32)


def _routing(aff, tri, *, cap):
    B, E, S = aff.shape
    blk = pl.BlockSpec((1, E, S), lambda b: (b, 0, 0))
    return pl.pallas_call(
        functools.partial(_route_kernel, cap=cap),
        grid=(B,),
        in_specs=[blk, pl.BlockSpec((S, S), lambda b: (0, 0))],
        out_specs=[blk, blk],
        out_shape=[jax.ShapeDtypeStruct((B, E, S), F32),
                   jax.ShapeDtypeStruct((B, E, S), jnp.int32)],
        compiler_params=_cparams("parallel"),
        name="ec_routing",
    )(aff, tri)


def _expert_kernel(h_ref, pos_ref, wgt_ref, wg_ref, wu_ref, wd_ref, x1_ref, gt_ref, acc_ref,
                   *, cap):
    e = pl.program_id(1)

    @pl.when(e == 0)
    def _():
        acc_ref[...] = jnp.zeros_like(acc_ref)

    slab = x1_ref.shape[1]
    rows = pl.ds(pl.multiple_of(e * slab, slab), slab)
    acc_ref[0, rows, :] += x1_ref[0]

    pos = pos_ref[0, 0]
    wgt = wgt_ref[0, 0]
    s = pos.shape[1]
    slot = lax.broadcasted_iota(jnp.int32, (cap, s), 0)
    hit = pos == slot
    onehot = jnp.where(hit, 1.0, 0.0).astype(BF16)
    vals = jnp.sum(jnp.where(hit, wgt, 0.0), axis=1, keepdims=True)
    hg = jnp.dot(onehot, h_ref[0], preferred_element_type=F32).astype(BF16)
    gate = jnp.dot(hg, wg_ref[0], preferred_element_type=F32)
    up = jnp.dot(hg, wu_ref[0], preferred_element_type=F32)
    hid = (gate * jax.nn.sigmoid(gate) * up).astype(BF16)
    y = jnp.dot(hid, wd_ref[0], preferred_element_type=F32) * vals * gt_ref[0]
    acc_ref[0] += lax.dot_general(onehot, y.astype(BF16), (((0,), (0,)), ((), ())),
                                  preferred_element_type=F32)


def _experts(h2, pos, wgt, w_gate, w_up, w_down, x1, gt2, *, cap):
    B, S, D = h2.shape
    E, _, F = w_gate.shape
    row = pl.BlockSpec((1, 1, 1, S), lambda b, e: (b, e, 0, 0))
    return pl.pallas_call(
        functools.partial(_expert_kernel, cap=cap),
        grid=(B, E),
        in_specs=[pl.BlockSpec((1, S, D), lambda b, e: (b, 0, 0)),
                  row, row,
                  pl.BlockSpec((1, D, F), lambda b, e: (e, 0, 0)),
                  pl.BlockSpec((1, D, F), lambda b, e: (e, 0, 0)),
                  pl.BlockSpec((1, F, D), lambda b, e: (e, 0, 0)),
                  pl.BlockSpec((1, S // E, D), lambda b, e: (b, e, 0)),
                  pl.BlockSpec((1, 1, D), lambda b, e: (b, 0, 0))],
        out_specs=pl.BlockSpec((1, S, D), lambda b, e: (b, 0, 0)),
        out_shape=jax.ShapeDtypeStruct((B, S, D), F32),
        compiler_params=_cparams("parallel", "arbitrary"),
        name="ec_experts",
    )(h2, pos.reshape(B, E, 1, S), wgt.reshape(B, E, 1, S), w_gate, w_up, w_down, x1, gt2)


def _rope_tables(seq):
    rows = seq // GRID_W
    row = jnp.repeat(jnp.arange(rows, dtype=F32), GRID_W)
    col = jnp.tile(jnp.arange(GRID_W, dtype=F32), rows)
    n_pairs_axis = HEAD_DIM // 4
    freqs = ROPE_THETA ** (-jnp.arange(n_pairs_axis, dtype=F32) / n_pairs_axis)
    ang = jnp.concatenate([row[:, None] * freqs, col[:, None] * freqs], axis=-1)
    cos = jnp.repeat(jnp.cos(ang), 2, axis=1)
    sin = jnp.repeat(jnp.sin(ang), 2, axis=1) * jnp.tile(jnp.array([-1.0, 1.0], F32), HEAD_DIM // 2)
    reps = LANES // HEAD_DIM
    return jnp.tile(cos, (1, reps)), jnp.tile(sin, (1, reps))


def _to_scan_layout(z, bh_pad):
    B, S, R = z.shape
    H = R // RWKV_HEAD
    zt = z.reshape(B, S, H, RWKV_HEAD).transpose(1, 3, 0, 2).reshape(S, RWKV_HEAD, B * H)
    if bh_pad != B * H:
        zt = jnp.pad(zt, ((0, 0), (0, 0), (0, bh_pad - B * H)))
    return zt


def _lane_param(p_hn, B, bh_pad):
    H, N = p_hn.shape
    t = jnp.tile(p_hn.T[:, None, :], (1, B, 1)).reshape(N, B * H)
    if bh_pad != B * H:
        t = jnp.pad(t, ((0, 0), (0, bh_pad - B * H)))
    return t


def _pick_tile(n, pref):
    t = min(n, pref)
    while n % t:
        t //= 2
    return t


def kernel(x, c, w_ada, b_ada, g_mix, w_in, q_norm, k_norm, mu_shift, w0, w_up, a0, a_up, g_up,
           k_k, k_a, r_k, ln_w, ln_b, w_out, g_ffn, w_router, w_gate, w_up_e, w_down):
    B, S, D = x.shape
    depth = w_ada.shape[0]
    rw = g_up.shape[2]
    H = rw // RWKV_HEAD
    cap = EC_CAPACITY * S // N_EXPERTS
    ts = _pick_tile(S, 256)
    tq = _pick_tile(S, 512)
    tc = _pick_tile(S, 64)
    bh = B * H
    bh_pad = -(-bh // LANES) * LANES

    cos_t, sin_t = _rope_tables(S)
    seg_n = max(w_in.shape[2] - 2 * ATT_KV_HEADS * HEAD_DIM
                - (3 * rw + DECAY_LORA + ICLR_LORA + GATE_LORA), ATT_KV_HEADS * HEAD_DIM)
    seg_id = np.arange(seg_n) // HEAD_DIM
    seg = jnp.asarray((seg_id[:, None] == seg_id[None, :]).astype(np.float32) / HEAD_DIM, BF16)
    tri = jnp.asarray(np.arange(S)[:, None] <= np.arange(S)[None, :], BF16)

    for l in range(depth):
        mod = _modulation(c, w_ada[l].astype(BF16), b_ada[l])
        sh1, sc1, gt1, sh2, sc2, gt2 = [m.reshape(B, 1, D) for m in jnp.split(mod, 6, axis=-1)]

        zeros = jnp.zeros((DECAY_LORA, rw), F32)
        wlo = jnp.concatenate([
            jnp.concatenate([w_up[l, 0], w_up[l, 1], zeros, zeros], axis=1),
            jnp.concatenate([zeros, zeros, a_up[l, 0], a_up[l, 1]], axis=1)], axis=0).astype(BF16)
        b0 = jnp.concatenate([w0[l, 0], w0[l, 1], a0[l, 0], a0[l, 1]]).reshape(1, 4 * rw)
        att_w = seg_n
        qg = jnp.tile(q_norm[l], att_w // HEAD_DIM).reshape(1, att_w)
        kg = jnp.tile(k_norm[l], ATT_KV_HEADS).reshape(1, ATT_KV_HEADS * HEAD_DIM)

        (q, k, v, r, kr, vr, wf, wb, af, ab, g) = _in_projection(
            x, sh1, sc1, g_mix[l].reshape(1, D), w_in[l].astype(BF16), seg, cos_t, sin_t, qg, kg,
            mu_shift[l].reshape(1, -1), wlo, b0, g_up[l].astype(BF16), ts=ts)

        o_att = _attention(q, k, v, tq=tq)

        rt, kt, vt, wft, wbt, aft, abt = [_to_scan_layout(z, bh_pad)
                                          for z in (r, kr, vr, wf, wb, af, ab)]
        kk_scale = _lane_param(k_k[l].reshape(H, RWKV_HEAD), B, bh_pad)
        ka = _lane_param(k_a[l].reshape(H, RWKV_HEAD), B, bh_pad)
        rk = _lane_param(r_k[l], B, bh_pad)
        yf, bf = _wkv_scan(rt, kt, vt, wft, aft, kk_scale, ka, rk, reverse=False, tc=tc)
        yb, bb = _wkv_scan(rt, kt, vt, wbt, abt, kk_scale, ka, rk, reverse=True, tc=tc)
        lw = _lane_param(ln_w[l].reshape(H, RWKV_HEAD), B, bh_pad)
        lb = _lane_param(ln_b[l].reshape(H, RWKV_HEAD), B, bh_pad)
        yo_t = _group_norm_bonus(yf, yb, bf, bb, vt, lw, lb, tc=tc)
        yo = (yo_t[:, :, :bh].reshape(S, RWKV_HEAD, B, H).transpose(2, 0, 3, 1)
              .reshape(B, S, rw))

        x1, h2, aff = _out_projection(o_att, yo, g, x, gt1, w_out[l].astype(BF16),
                                      g_ffn[l].reshape(1, D), sh2, sc2, w_router[l].T,
                                      ts=_pick_tile(S, 512))
        wgt, pos = _routing(aff, tri, cap=cap)
        x = _experts(h2, pos, wgt, w_gate[l].astype(BF16), w_up_e[l].astype(BF16),
                     w_down[l].astype(BF16), x1, gt2, cap=cap)
    return x
```

```python
import functools

import jax
import jax.numpy as jnp
import numpy as np
from jax import lax
from jax.experimental import pallas as pl
from jax.experimental.pallas import tpu as pltpu

LANES = 128
SUBLANES = 8
VMEM_LIMIT_BYTES = 56 * 1024 * 1024

GRID_W = 64
ROPE_THETA = 10000.0
HEAD_DIM = 64
ATT_KV_HEADS = 2
RWKV_HEAD = 64
DECAY_LORA = 64
ICLR_LORA = 64
GATE_LORA = 128
DECAY_SCALE = 0.606531
GN_EPS = 64e-5
NORM_EPS = 1e-6
L2_EPS = 1e-12
N_EXPERTS = 16
EC_CAPACITY = 2
LOG2_E = 1.4426950408889634

F32 = jnp.float32
BF16 = jnp.bfloat16


def _cparams(*sem):
    return pltpu.CompilerParams(dimension_semantics=sem, vmem_limit_bytes=VMEM_LIMIT_BYTES)


def _bdot(a, b):
    return jnp.dot(a.astype(BF16), b.astype(BF16), preferred_element_type=F32)


def _split_dot(a, b_bf16):
    hi = a.astype(BF16)
    lo = (a - hi.astype(F32)).astype(BF16)
    return (jnp.dot(hi, b_bf16, preferred_element_type=F32)
            + jnp.dot(lo, b_bf16, preferred_element_type=F32))


def _rms_rows(x):
    return x * lax.rsqrt(jnp.mean(x * x, axis=-1, keepdims=True) + NORM_EPS)


def _mod_kernel(c_ref, w_ref, b_ref, o_ref):
    c = c_ref[...]
    cond = c * jax.nn.sigmoid(c)
    o_ref[...] = _bdot(cond, w_ref[...]) + b_ref[...]


def _modulation(c, w_ada, b_ada):
    B, D = c.shape
    n_out = w_ada.shape[1]
    tn = D
    return pl.pallas_call(
        _mod_kernel,
        grid=(n_out // tn,),
        in_specs=[pl.BlockSpec((B, D), lambda j: (0, 0)),
                  pl.BlockSpec((D, tn), lambda j: (0, j)),
                  pl.BlockSpec((1, tn), lambda j: (0, j))],
        out_specs=pl.BlockSpec((B, tn), lambda j: (0, j)),
        out_shape=jax.ShapeDtypeStruct((B, n_out), F32),
        compiler_params=_cparams("arbitrary"),
        name="adaln_mod",
    )(c, w_ada, b_ada.reshape(1, n_out))


def _rope(x, cos, sin_signed):
    n = x.shape[-1]
    lane = lax.broadcasted_iota(jnp.int32, x.shape, 1)
    nxt = pltpu.roll(x, n - 1, axis=1)
    prv = pltpu.roll(x, 1, axis=1)
    swapped = jnp.where(lane % 2 == 0, nxt, prv)
    return x * cos + swapped * sin_signed


def _inproj_kernel(x_ref, xp_ref, xn_ref, sh_ref, sc_ref, gm_ref, w_ref, seg_ref, cos_ref,
                   sin_ref, qg_ref, kg_ref, mu_ref, wlo_ref, b0_ref, gup_ref,
                   q_out, k_out, v_out, r_out, kr_out, vr_out, wf_out, wb_out, af_out,
                   ab_out, g_out, *, att_w, kv_w, rw):
    i = pl.program_id(1)
    n_i = pl.num_programs(1)
    ts = x_ref.shape[1]
    gain = gm_ref[...]
    scale = 1.0 + sc_ref[0]
    shift = sh_ref[0]

    def norm_mod(xv):
        return (_rms_rows(xv) * gain * scale + shift).astype(BF16)

    halo = jnp.concatenate([xp_ref[0], xn_ref[0]], axis=0)
    h = jnp.concatenate([norm_mod(x_ref[0]), norm_mod(halo)], axis=0)
    p_all = jnp.dot(h, w_ref[...], preferred_element_type=F32)
    p = p_all[:ts]

    seg = seg_ref[...]

    def qk_norm_rope(z, gain_row, reps):
        ms = _split_dot(z * z, seg[: z.shape[1], : z.shape[1]])
        zn = z * lax.rsqrt(ms + NORM_EPS) * gain_row
        cos = jnp.concatenate([cos_ref[...]] * reps, axis=1) if reps > 1 else cos_ref[...]
        sin = jnp.concatenate([sin_ref[...]] * reps, axis=1) if reps > 1 else sin_ref[...]
        return _rope(zn, cos, sin)

    q = qk_norm_rope(p[:, :att_w], qg_ref[...], att_w // LANES)
    q_out[0] = (q * (HEAD_DIM ** -0.5 * LOG2_E)).astype(BF16)
    k = qk_norm_rope(p[:, att_w:att_w + kv_w], kg_ref[...], kv_w // LANES)
    k_out[0] = k.astype(BF16)
    v_out[0] = p[:, att_w + kv_w:att_w + 2 * kv_w].astype(BF16)

    off = att_w + 2 * kv_w
    pr = p[:, off:]
    ph = p_all[ts:, off:]
    prev_row = jnp.where(i == 0, 0.0, ph[SUBLANES - 1:SUBLANES, :])
    next_row = jnp.where(i == n_i - 1, 0.0, ph[SUBLANES:SUBLANES + 1, :])
    row = lax.broadcasted_iota(jnp.int32, pr.shape, 0)
    prev = jnp.where(row == 0, prev_row, pltpu.roll(pr, 1, axis=0))
    nxt = jnp.where(row == ts - 1, next_row, pltpu.roll(pr, ts - 1, axis=0))
    ps = pr + mu_ref[...] * (0.5 * (prev + nxt) - pr)

    r_out[0] = ps[:, :rw]
    kr_out[0] = ps[:, rw:2 * rw]
    vr_out[0] = ps[:, 2 * rw:3 * rw]
    lo_in = ps[:, 3 * rw:3 * rw + DECAY_LORA + ICLR_LORA]
    lane = lax.broadcasted_iota(jnp.int32, lo_in.shape, 1)
    lo_in = jnp.where(lane < DECAY_LORA, jnp.tanh(lo_in), lo_in)
    lo = _bdot(lo_in, wlo_ref[...]) + b0_ref[...]
    wf_out[0] = jnp.exp(-DECAY_SCALE * jax.nn.sigmoid(lo[:, :rw]))
    wb_out[0] = jnp.exp(-DECAY_SCALE * jax.nn.sigmoid(lo[:, rw:2 * rw]))
    af_out[0] = jax.nn.sigmoid(lo[:, 2 * rw:3 * rw])
    ab_out[0] = jax.nn.sigmoid(lo[:, 3 * rw:])
    dg = ps[:, 3 * rw + DECAY_LORA + ICLR_LORA:]
    g_out[0] = _bdot(jax.nn.sigmoid(dg), gup_ref[...])


def _in_projection(x, sh1, sc1, g_mix, w_in, seg, cos_t, sin_t, q_norm, k_norm, mu_shift,
                   wlo, b0, g_up, *, ts):
    B, S, D = x.shape
    n_in = w_in.shape[1]
    rw = g_up.shape[1]
    kv_w = ATT_KV_HEADS * HEAD_DIM
    att_w = n_in - 2 * kv_w - (3 * rw + DECAY_LORA + ICLR_LORA + GATE_LORA)
    n_t = S // ts
    hb = ts // SUBLANES
    n_hb = S // SUBLANES
    const = lambda b, i: (0, 0)
    tile = lambda b, i: (b, i, 0)
    modspec = pl.BlockSpec((1, 1, D), lambda b, i: (b, 0, 0))
    in_specs = [
        pl.BlockSpec((1, ts, D), tile),
        pl.BlockSpec((1, SUBLANES, D), lambda b, i: (b, jnp.maximum(i * hb - 1, 0), 0)),
        pl.BlockSpec((1, SUBLANES, D), lambda b, i: (b, jnp.minimum((i + 1) * hb, n_hb - 1), 0)),
        modspec, modspec,
        pl.BlockSpec((1, D), const),
        pl.BlockSpec((D, n_in), const),
        pl.BlockSpec(seg.shape, const),
        pl.BlockSpec((ts, LANES), lambda b, i: (i, 0)),
        pl.BlockSpec((ts, LANES), lambda b, i: (i, 0)),
        pl.BlockSpec((1, att_w), const),
        pl.BlockSpec((1, kv_w), const),
        pl.BlockSpec((1, mu_shift.shape[1]), const),
        pl.BlockSpec(wlo.shape, const),
        pl.BlockSpec(b0.shape, const),
        pl.BlockSpec(g_up.shape, const),
    ]
    out_shapes = ([jax.ShapeDtypeStruct((B, S, att_w), BF16),
                   jax.ShapeDtypeStruct((B, S, kv_w), BF16),
                   jax.ShapeDtypeStruct((B, S, kv_w), BF16)]
                  + [jax.ShapeDtypeStruct((B, S, rw), F32)] * 8)
    out_specs = ([pl.BlockSpec((1, ts, att_w), tile),
                  pl.BlockSpec((1, ts, kv_w), tile),
                  pl.BlockSpec((1, ts, kv_w), tile)]
                 + [pl.BlockSpec((1, ts, rw), tile)] * 8)
    return pl.pallas_call(
        functools.partial(_inproj_kernel, att_w=att_w, kv_w=kv_w, rw=rw),
        grid=(B, n_t),
        in_specs=in_specs,
        out_specs=out_specs,
        out_shape=out_shapes,
        compiler_params=_cparams("parallel", "arbitrary"),
        name="in_projection",
    )(x, x, x, sh1, sc1, g_mix, w_in, seg, cos_t, sin_t, q_norm, k_norm, mu_shift, wlo, b0, g_up)


def _attn_kernel(q_ref, k_ref, v_ref, o_ref):
    k = k_ref[0, 0]
    v = v_ref[0, 0]
    grp = q_ref.shape[2] // HEAD_DIM
    outs = []
    for hh in range(grp):
        q = q_ref[0, :, hh * HEAD_DIM:(hh + 1) * HEAD_DIM]
        s = lax.dot_general(q, k, (((1,), (1,)), ((), ())), preferred_element_type=F32)
        e = jnp.exp2(s - jnp.max(s, axis=-1, keepdims=True)).astype(BF16)
        o = jnp.dot(e, v, preferred_element_type=F32)
        outs.append(o[:, :HEAD_DIM] / o[:, HEAD_DIM:HEAD_DIM + 1])
    o_ref[0] = jnp.concatenate(outs, axis=1).astype(o_ref.dtype)


def _attention(q, k, v, *, tq):
    B, S, att_w = q.shape
    gw = att_w // ATT_KV_HEADS
    k4 = k.reshape(B, S, ATT_KV_HEADS, HEAD_DIM).transpose(0, 2, 1, 3)
    v4 = v.reshape(B, S, ATT_KV_HEADS, HEAD_DIM).transpose(0, 2, 1, 3)
    v4 = jnp.concatenate([v4, jnp.ones_like(v4)], axis=-1)
    return pl.pallas_call(
        _attn_kernel,
        grid=(B, ATT_KV_HEADS, S // tq),
        in_specs=[pl.BlockSpec((1, tq, gw), lambda b, g, i: (b, i, g)),
                  pl.BlockSpec((1, 1, S, HEAD_DIM), lambda b, g, i: (b, g, 0, 0)),
                  pl.BlockSpec((1, 1, S, 2 * HEAD_DIM), lambda b, g, i: (b, g, 0, 0))],
        out_specs=pl.BlockSpec((1, tq, gw), lambda b, g, i: (b, i, g)),
        out_shape=jax.ShapeDtypeStruct((B, S, att_w), BF16),
        compiler_params=_cparams("parallel", "parallel", "arbitrary"),
        name="attention",
    )(q, k4, v4)


def _wkv_kernel(r_ref, k_ref, v_ref, w_ref, a_ref, kk_scale_ref, ka_ref, rk_ref,
                y_ref, bs_ref, s_ref, g_ref, ca_ref, cb_ref, cr_ref, cn_ref, *, reverse):
    c = pl.program_id(1)
    tc, n, _ = r_ref.shape

    @pl.when(c == 0)
    def _():
        s_ref[...] = jnp.zeros_like(s_ref)

    def tidx(j):
        return (tc - 1 - j) if reverse else j

    def prepare(j, g):
        t = tidx(j)
        kt = k_ref[t]
        a = a_ref[t]
        r = r_ref[t]
        kk = kt * kk_scale_ref[...]
        kk = kk / jnp.maximum(jnp.sqrt(jnp.sum(kk * kk, axis=0, keepdims=True)), L2_EPS)
        kd = kt * (1.0 + (a - 1.0) * ka_ref[...])
        bs_ref[t] = jnp.sum(r * kd * rk_ref[...], axis=0, keepdims=True)
        cn_ref[t] = -(kk * g)
        g = g * w_ref[t]
        g_inv = 1.0 / g
        ca_ref[t] = kk * a * g_inv
        cb_ref[t] = kd * g_inv
        cr_ref[t] = r * g
        return g

    g_ref[...] = lax.fori_loop(0, tc, prepare, jnp.ones((n, LANES), F32), unroll=4)

    t0 = tidx(0)
    sa0 = jnp.zeros((n, LANES), F32)
    for kx in range(n):
        sa0 = sa0 + s_ref[kx] * cn_ref[t0, kx:kx + 1, :]

    def step(j, sa):
        t = tidx(j)
        tn = tidx(jnp.minimum(j + 1, tc - 1))
        v_t = v_ref[t]
        y = jnp.zeros((n, LANES), F32)
        sa_next = jnp.zeros((n, LANES), F32)
        for kx in range(n):
            row = pl.ds(kx, 1)
            z_new = s_ref[kx] + sa * ca_ref[t, row, :] + v_t * cb_ref[t, row, :]
            s_ref[kx] = z_new
            y = y + z_new * cr_ref[t, row, :]
            sa_next = sa_next + z_new * cn_ref[tn, row, :]
        y_ref[t] = y
        return sa_next

    lax.fori_loop(0, tc, step, sa0)

    for kx in range(n):
        s_ref[kx] = s_ref[kx] * g_ref[kx:kx + 1, :]


def _wkv_scan(r, k, v, w, a, kk_scale, ka, rk, *, reverse, tc):
    S, N, BH = r.shape
    n_c = S // tc
    n_l = BH // LANES
    if reverse:
        tmap = lambda j, c: (n_c - 1 - c, 0, j)
    else:
        tmap = lambda j, c: (c, 0, j)
    blk = pl.BlockSpec((tc, N, LANES), tmap)
    par = pl.BlockSpec((N, LANES), lambda j, c: (0, j))
    return pl.pallas_call(
        functools.partial(_wkv_kernel, reverse=reverse),
        grid=(n_l, n_c),
        in_specs=[blk] * 5 + [par] * 3,
        out_specs=[blk, pl.BlockSpec((tc, 1, LANES), tmap)],
        out_shape=[jax.ShapeDtypeStruct((S, N, BH), F32),
                   jax.ShapeDtypeStruct((S, 1, BH), F32)],
        scratch_shapes=([pltpu.VMEM((N, N, LANES), F32), pltpu.VMEM((N, LANES), F32)]
                        + [pltpu.VMEM((tc, N, LANES), F32)] * 4),
        compiler_params=_cparams("parallel", "arbitrary"),
        name="wkv7_scan_bwd" if reverse else "wkv7_scan_fwd",
    )(r, k, v, w, a, kk_scale, ka, rk)


def _gn_kernel(yf_ref, yb_ref, bf_ref, bb_ref, v_ref, lw_ref, lb_ref, o_ref):
    y = yf_ref[...] + yb_ref[...]
    mean = jnp.mean(y, axis=1, keepdims=True)
    d = y - mean
    var = jnp.mean(d * d, axis=1, keepdims=True)
    o_ref[...] = (d * lax.rsqrt(var + GN_EPS) * lw_ref[...] + lb_ref[...]
                  + (bf_ref[...] + bb_ref[...]) * v_ref[...])


def _group_norm_bonus(yf, yb, bf, bb, v, lw, lb, *, tc):
    S, N, BH = yf.shape
    blk = pl.BlockSpec((tc, N, BH), lambda c: (c, 0, 0))
    sblk = pl.BlockSpec((tc, 1, BH), lambda c: (c, 0, 0))
    par = pl.BlockSpec((N, BH), lambda c: (0, 0))
    return pl.pallas_call(
        _gn_kernel,
        grid=(S // tc,),
        in_specs=[blk, blk, sblk, sblk, blk, par, par],
        out_specs=blk,
        out_shape=jax.ShapeDtypeStruct((S, N, BH), F32),
        compiler_params=_cparams("parallel"),
        name="wkv7_groupnorm",
    )(yf, yb, bf, bb, v, lw, lb)


def _outproj_kernel(oa_ref, yo_ref, g_ref, x_ref, gt_ref, wo_ref, gf_ref, sh_ref, sc_ref,
                    wr_ref, x1_out, h2_out, aff_out):
    mix = jnp.concatenate([oa_ref[0], (yo_ref[0] * g_ref[0]).astype(BF16)], axis=1)
    x1 = x_ref[0] + gt_ref[0] * jnp.dot(mix, wo_ref[...], preferred_element_type=F32)
    x1_out[0] = x1
    h2 = _rms_rows(x1) * gf_ref[...] * (1.0 + sc_ref[0]) + sh_ref[0]
    h2_out[0] = h2.astype(BF16)
    wr = wr_ref[...]
    nt = (((1,), (1,)), ((), ()))
    w_hi = wr.astype(BF16)
    w_lo = (wr - w_hi.astype(F32)).astype(BF16)
    h_hi = h2.astype(BF16)
    h_lo = (h2 - h_hi.astype(F32)).astype(BF16)
    logits = (lax.dot_general(w_hi, h_hi, nt, preferred_element_type=F32)
              + lax.dot_general(w_hi, h_lo, nt, preferred_element_type=F32)
              + lax.dot_general(w_lo, h_hi, nt, preferred_element_type=F32))
    m = jnp.max(logits, axis=0, keepdims=True)
    e = jnp.exp(logits - m)
    aff_out[0] = e / jnp.sum(e, axis=0, keepdims=True)


def _out_projection(o_att, yo, g, x, gt1, w_out, g_ffn, sh2, sc2, w_router_t, *, ts):
    B, S, D = x.shape
    aw = o_att.shape[2]
    rw = yo.shape[2]
    E = w_router_t.shape[0]
    tile = lambda b, i: (b, i, 0)
    const = lambda b, i: (0, 0)
    modspec = pl.BlockSpec((1, 1, D), lambda b, i: (b, 0, 0))
    return pl.pallas_call(
        _outproj_kernel,
        grid=(B, S // ts),
        in_specs=[pl.BlockSpec((1, ts, aw), tile),
                  pl.BlockSpec((1, ts, rw), tile),
                  pl.BlockSpec((1, ts, rw), tile),
                  pl.BlockSpec((1, ts, D), tile),
                  modspec,
                  pl.BlockSpec((aw + rw, D), const),
                  pl.BlockSpec((1, D), const),
                  modspec, modspec,
                  pl.BlockSpec((E, D), const)],
        out_specs=[pl.BlockSpec((1, ts, D), tile),
                   pl.BlockSpec((1, ts, D), tile),
                   pl.BlockSpec((1, E, ts), lambda b, i: (b, 0, i))],
        out_shape=[jax.ShapeDtypeStruct((B, S, D), F32),
                   jax.ShapeDtypeStruct((B, S, D), BF16),
                   jax.ShapeDtypeStruct((B, E, S), F32)],
        compiler_params=_cparams("parallel", "arbitrary"),
        name="out_projection",
    )(o_att, yo, g, x, gt1, w_out, g_ffn, sh2, sc2, w_router_t)


def _route_kernel(aff_ref, tri_ref, wgt_out, pos_out, *, cap):
    aff = aff_ref[...]
    capf = jnp.float32(cap)

    def count(mask):
        return jnp.sum(jnp.where(mask, 1.0, 0.0), axis=1, keepdims=True)

    def as_float(word):
        return pltpu.bitcast(word, F32)

    def bit_step(i, thr):
        cand = thr | jnp.left_shift(jnp.int32(1), 30 - i)
        return jnp.where(count(aff >= as_float(cand)) >= capf, cand, thr)

    thr = lax.fori_loop(0, 31, bit_step, jnp.zeros((aff.shape[0], 1), jnp.int32))
    thr_f = as_float(thr)
    gt = aff > thr_f
    eq = aff == thr_f
    need = capf - count(gt)
    tri = tri_ref[...]
    eq_rank = jnp.dot(jnp.where(eq, 1.0, 0.0).astype(BF16), tri, preferred_element_type=F32)
    sel = jnp.where(gt, 1.0, jnp.where(eq, jnp.where(eq_rank <= need, 1.0, 0.0), 0.0))
    rank = jnp.dot(sel.astype(BF16), tri, preferred_element_type=F32)
    chosen = sel > 0.5
    wgt_out[...] = jnp.where(chosen, aff, 0.0)
    pos_out[...] = jnp.where(chosen, rank - 1.0, -1.0).astype(jnp.int32)


def _routing(aff, tri, *, cap):
    B, E, S = aff.shape
    rows = _pick_tile(B * E, LANES)
    blk = pl.BlockSpec((rows, S), lambda i: (i, 0))
    wgt, pos = pl.pallas_call(
        functools.partial(_route_kernel, cap=cap),
        grid=(B * E // rows,),
        in_specs=[blk, pl.BlockSpec((S, S), lambda i: (0, 0))],
        out_specs=[blk, blk],
        out_shape=[jax.ShapeDtypeStruct((B * E, S), F32),
                   jax.ShapeDtypeStruct((B * E, S), jnp.int32)],
        compiler_params=_cparams("parallel"),
        name="ec_routing",
    )(aff.reshape(B * E, S), tri)
    return wgt.reshape(B, E, S), pos.reshape(B, E, S)


def _expert_kernel(h_ref, pos_ref, wgt_ref, wg_ref, wu_ref, wd_ref, x1_ref, gt_ref, acc_ref,
                   *, cap):
    e = pl.program_id(1)

    @pl.when(e == 0)
    def _():
        acc_ref[...] = jnp.zeros_like(acc_ref)

    slab = x1_ref.shape[1]
    rows = pl.ds(pl.multiple_of(e * slab, slab), slab)
    acc_ref[0, rows, :] += x1_ref[0]

    pos = pos_ref[0, 0]
    wgt = wgt_ref[0, 0]
    s = pos.shape[1]
    slot = lax.broadcasted_iota(jnp.int32, (cap, s), 0)
    hit = pos == slot
    onehot = jnp.where(hit, 1.0, 0.0).astype(BF16)
    vals = jnp.sum(jnp.where(hit, wgt, 0.0), axis=1, keepdims=True)
    hg = jnp.dot(onehot, h_ref[0], preferred_element_type=F32).astype(BF16)
    gate = jnp.dot(hg, wg_ref[0], preferred_element_type=F32)
    up = jnp.dot(hg, wu_ref[0], preferred_element_type=F32)
    hid = (gate * jax.nn.sigmoid(gate) * up).astype(BF16)
    y = jnp.dot(hid, wd_ref[0], preferred_element_type=F32) * vals * gt_ref[0]
    acc_ref[0] += lax.dot_general(onehot, y.astype(BF16), (((0,), (0,)), ((), ())),
                                  preferred_element_type=F32)


def _experts(h2, pos, wgt, w_gate, w_up, w_down, x1, gt2, *, cap):
    B, S, D = h2.shape
    E, _, F = w_gate.shape
    row = pl.BlockSpec((1, 1, 1, S), lambda b, e: (b, e, 0, 0))
    return pl.pallas_call(
        functools.partial(_expert_kernel, cap=cap),
        grid=(B, E),
        in_specs=[pl.BlockSpec((1, S, D), lambda b, e: (b, 0, 0)),
                  row, row,
                  pl.BlockSpec((1, D, F), lambda b, e: (e, 0, 0)),
                  pl.BlockSpec((1, D, F), lambda b, e: (e, 0, 0)),
                  pl.BlockSpec((1, F, D), lambda b, e: (e, 0, 0)),
                  pl.BlockSpec((1, S // E, D), lambda b, e: (b, e, 0)),
                  pl.BlockSpec((1, 1, D), lambda b, e: (b, 0, 0))],
        out_specs=pl.BlockSpec((1, S, D), lambda b, e: (b, 0, 0)),
        out_shape=jax.ShapeDtypeStruct((B, S, D), F32),
        compiler_params=_cparams("parallel", "arbitrary"),
        name="ec_experts",
    )(h2, pos.reshape(B, E, 1, S), wgt.reshape(B, E, 1, S), w_gate, w_up, w_down, x1, gt2)


def _rope_tables(seq):
    rows = seq // GRID_W
    row = jnp.repeat(jnp.arange(rows, dtype=F32), GRID_W)
    col = jnp.tile(jnp.arange(GRID_W, dtype=F32), rows)
    n_pairs_axis = HEAD_DIM // 4
    freqs = ROPE_THETA ** (-jnp.arange(n_pairs_axis, dtype=F32) / n_pairs_axis)
    ang = jnp.concatenate([row[:, None] * freqs, col[:, None] * freqs], axis=-1)
    cos = jnp.repeat(jnp.cos(ang), 2, axis=1)
    sin = jnp.repeat(jnp.sin(ang), 2, axis=1) * jnp.tile(jnp.array([-1.0, 1.0], F32), HEAD_DIM // 2)
    reps = LANES // HEAD_DIM
    return jnp.tile(cos, (1, reps)), jnp.tile(sin, (1, reps))


def _to_scan_layout(z, bh_pad):
    B, S, R = z.shape
    H = R // RWKV_HEAD
    zt = z.reshape(B, S, H, RWKV_HEAD).transpose(1, 3, 0, 2).reshape(S, RWKV_HEAD, B * H)
    if bh_pad != B * H:
        zt = jnp.pad(zt, ((0, 0), (0, 0), (0, bh_pad - B * H)))
    return zt


def _lane_param(p_hn, B, bh_pad):
    H, N = p_hn.shape
    t = jnp.tile(p_hn.T[:, None, :], (1, B, 1)).reshape(N, B * H)
    if bh_pad != B * H:
        t = jnp.pad(t, ((0, 0), (0, bh_pad - B * H)))
    return t


def _pick_tile(n, pref):
    t = min(n, pref)
    while n % t:
        t //= 2
    return t


def kernel(x, c, w_ada, b_ada, g_mix, w_in, q_norm, k_norm, mu_shift, w0, w_up, a0, a_up, g_up,
           k_k, k_a, r_k, ln_w, ln_b, w_out, g_ffn, w_router, w_gate, w_up_e, w_down):
    B, S, D = x.shape
    depth = w_ada.shape[0]
    rw = g_up.shape[2]
    H = rw // RWKV_HEAD
    cap = EC_CAPACITY * S // N_EXPERTS
    ts = _pick_tile(S, 512)
    tq = _pick_tile(S, 512)
    tc = _pick_tile(S, 64)
    bh = B * H
    bh_pad = -(-bh // LANES) * LANES

    cos_t, sin_t = _rope_tables(S)
    seg_n = max(w_in.shape[2] - 2 * ATT_KV_HEADS * HEAD_DIM
                - (3 * rw + DECAY_LORA + ICLR_LORA + GATE_LORA), ATT_KV_HEADS * HEAD_DIM)
    seg_id = np.arange(seg_n) // HEAD_DIM
    seg = jnp.asarray((seg_id[:, None] == seg_id[None, :]).astype(np.float32) / HEAD_DIM, BF16)
    tri = jnp.asarray(np.arange(S)[:, None] <= np.arange(S)[None, :], BF16)

    for l in range(depth):
        mod = _modulation(c, w_ada[l].astype(BF16), b_ada[l])
        sh1, sc1, gt1, sh2, sc2, gt2 = [m.reshape(B, 1, D) for m in jnp.split(mod, 6, axis=-1)]

        zeros = jnp.zeros((DECAY_LORA, rw), F32)
        wlo = jnp.concatenate([
            jnp.concatenate([w_up[l, 0], w_up[l, 1], zeros, zeros], axis=1),
            jnp.concatenate([zeros, zeros, a_up[l, 0], a_up[l, 1]], axis=1)], axis=0).astype(BF16)
        b0 = jnp.concatenate([w0[l, 0], w0[l, 1], a0[l, 0], a0[l, 1]]).reshape(1, 4 * rw)
        att_w = seg_n
        qg = jnp.tile(q_norm[l], att_w // HEAD_DIM).reshape(1, att_w)
        kg = jnp.tile(k_norm[l], ATT_KV_HEADS).reshape(1, ATT_KV_HEADS * HEAD_DIM)

        (q, k, v, r, kr, vr, wf, wb, af, ab, g) = _in_projection(
            x, sh1, sc1, g_mix[l].reshape(1, D), w_in[l].astype(BF16), seg, cos_t, sin_t, qg, kg,
            mu_shift[l].reshape(1, -1), wlo, b0, g_up[l].astype(BF16), ts=ts)

        o_att = _attention(q, k, v, tq=tq)

        rt, kt, vt, wft, wbt, aft, abt = [_to_scan_layout(z, bh_pad)
                                          for z in (r, kr, vr, wf, wb, af, ab)]
        kk_scale = _lane_param(k_k[l].reshape(H, RWKV_HEAD), B, bh_pad)
        ka = _lane_param(k_a[l].reshape(H, RWKV_HEAD), B, bh_pad)
        rk = _lane_param(r_k[l], B, bh_pad)
        yf, bf = _wkv_scan(rt, kt, vt, wft, aft, kk_scale, ka, rk, reverse=False, tc=tc)
        yb, bb = _wkv_scan(rt, kt, vt, wbt, abt, kk_scale, ka, rk, reverse=True, tc=tc)
        lw = _lane_param(ln_w[l].reshape(H, RWKV_HEAD), B, bh_pad)
        lb = _lane_param(ln_b[l].reshape(H, RWKV_HEAD), B, bh_pad)
        yo_t = _group_norm_bonus(yf, yb, bf, bb, vt, lw, lb, tc=tc)
        yo = (yo_t[:, :, :bh].reshape(S, RWKV_HEAD, B, H).transpose(2, 0, 3, 1)
              .reshape(B, S, rw))

        x1, h2, aff = _out_projection(o_att, yo, g, x, gt1, w_out[l].astype(BF16),
                                      g_ffn[l].reshape(1, D), sh2, sc2, w_router[l].T,
                                      ts=_pick_tile(S, 512))
        wgt, pos = _routing(aff, tri, cap=cap)
        x = _experts(h2, pos, wgt, w_gate[l].astype(BF16), w_up_e[l].astype(BF16),
                     w_down[l].astype(BF16), x1, gt2, cap=cap)
    return x
```

```python
import functools

import jax
import jax.numpy as jnp
import numpy as np
from jax import lax
from jax.experimental import pallas as pl
from jax.experimental.pallas import tpu as pltpu

LANES = 128
SUBLANES = 8
VMEM_LIMIT_BYTES = 56 * 1024 * 1024

GRID_W = 64
ROPE_THETA = 10000.0
HEAD_DIM = 64
ATT_KV_HEADS = 2
RWKV_HEAD = 64
DECAY_LORA = 64
ICLR_LORA = 64
GATE_LORA = 128
DECAY_SCALE = 0.606531
GN_EPS = 64e-5
NORM_EPS = 1e-6
L2_EPS = 1e-12
N_EXPERTS = 16
EC_CAPACITY = 2
LOG2_E = 1.4426950408889634

F32 = jnp.float32
BF16 = jnp.bfloat16


def _cparams(*sem):
    return pltpu.CompilerParams(dimension_semantics=sem, vmem_limit_bytes=VMEM_LIMIT_BYTES)


def _bdot(a, b):
    return jnp.dot(a.astype(BF16), b.astype(BF16), preferred_element_type=F32)


def _split_dot(a, b_bf16):
    hi = a.astype(BF16)
    lo = (a - hi.astype(F32)).astype(BF16)
    return (jnp.dot(hi, b_bf16, preferred_element_type=F32)
            + jnp.dot(lo, b_bf16, preferred_element_type=F32))


def _rms_rows(x):
    return x * lax.rsqrt(jnp.mean(x * x, axis=-1, keepdims=True) + NORM_EPS)


def _mod_kernel(c_ref, w_ref, b_ref, o_ref):
    c = c_ref[...]
    cond = c * jax.nn.sigmoid(c)
    o_ref[...] = _bdot(cond, w_ref[...]) + b_ref[...]


def _modulation(c, w_ada, b_ada):
    B, D = c.shape
    n_out = w_ada.shape[1]
    tn = D
    return pl.pallas_call(
        _mod_kernel,
        grid=(n_out // tn,),
        in_specs=[pl.BlockSpec((B, D), lambda j: (0, 0)),
                  pl.BlockSpec((D, tn), lambda j: (0, j)),
                  pl.BlockSpec((1, tn), lambda j: (0, j))],
        out_specs=pl.BlockSpec((B, tn), lambda j: (0, j)),
        out_shape=jax.ShapeDtypeStruct((B, n_out), F32),
        compiler_params=_cparams("arbitrary"),
        name="adaln_mod",
    )(c, w_ada, b_ada.reshape(1, n_out))


def _rope(x, cos, sin_signed):
    n = x.shape[-1]
    lane = lax.broadcasted_iota(jnp.int32, x.shape, 1)
    nxt = pltpu.roll(x, n - 1, axis=1)
    prv = pltpu.roll(x, 1, axis=1)
    swapped = jnp.where(lane % 2 == 0, nxt, prv)
    return x * cos + swapped * sin_signed


def _inproj_kernel(x_ref, xp_ref, xn_ref, sh_ref, sc_ref, gm_ref, w_ref, seg_ref, cos_ref,
                   sin_ref, qg_ref, kg_ref, mu_ref, wlo_ref, b0_ref, gup_ref,
                   q_out, k_out, v_out, r_out, kr_out, vr_out, wf_out, wb_out, af_out,
                   ab_out, g_out, *, att_w, kv_w, rw):
    i = pl.program_id(1)
    n_i = pl.num_programs(1)
    ts = x_ref.shape[1]
    gain = gm_ref[...]
    scale = 1.0 + sc_ref[0]
    shift = sh_ref[0]

    def norm_mod(xv):
        return (_rms_rows(xv) * gain * scale + shift).astype(BF16)

    halo = jnp.concatenate([xp_ref[0], xn_ref[0]], axis=0)
    h = jnp.concatenate([norm_mod(x_ref[0]), norm_mod(halo)], axis=0)
    p_all = jnp.dot(h, w_ref[...], preferred_element_type=F32)
    p = p_all[:ts]

    seg = seg_ref[...]

    def qk_norm_rope(z, gain_row, reps):
        ms = _split_dot(z * z, seg[: z.shape[1], : z.shape[1]])
        zn = z * lax.rsqrt(ms + NORM_EPS) * gain_row
        cos = jnp.concatenate([cos_ref[...]] * reps, axis=1) if reps > 1 else cos_ref[...]
        sin = jnp.concatenate([sin_ref[...]] * reps, axis=1) if reps > 1 else sin_ref[...]
        return _rope(zn, cos, sin)

    q = qk_norm_rope(p[:, :att_w], qg_ref[...], att_w // LANES)
    q_out[0] = (q * (HEAD_DIM ** -0.5 * LOG2_E)).astype(BF16)
    k = qk_norm_rope(p[:, att_w:att_w + kv_w], kg_ref[...], kv_w // LANES)
    k_out[0] = k.astype(BF16)
    v_out[0] = p[:, att_w + kv_w:att_w + 2 * kv_w].astype(BF16)

    off = att_w + 2 * kv_w
    pr = p[:, off:]
    ph = p_all[ts:, off:]
    prev_row = jnp.where(i == 0, 0.0, ph[SUBLANES - 1:SUBLANES, :])
    next_row = jnp.where(i == n_i - 1, 0.0, ph[SUBLANES:SUBLANES + 1, :])
    row = lax.broadcasted_iota(jnp.int32, pr.shape, 0)
    prev = jnp.where(row == 0, prev_row, pltpu.roll(pr, 1, axis=0))
    nxt = jnp.where(row == ts - 1, next_row, pltpu.roll(pr, ts - 1, axis=0))
    ps = pr + mu_ref[...] * (0.5 * (prev + nxt) - pr)

    r_out[0] = ps[:, :rw].T
    kr_out[0] = ps[:, rw:2 * rw].T
    vr_out[0] = ps[:, 2 * rw:3 * rw].T
    lo_in = ps[:, 3 * rw:3 * rw + DECAY_LORA + ICLR_LORA]
    lane = lax.broadcasted_iota(jnp.int32, lo_in.shape, 1)
    lo_in = jnp.where(lane < DECAY_LORA, jnp.tanh(lo_in), lo_in)
    lo = _bdot(lo_in, wlo_ref[...]) + b0_ref[...]
    wf_out[0] = jnp.exp(-DECAY_SCALE * jax.nn.sigmoid(lo[:, :rw])).T
    wb_out[0] = jnp.exp(-DECAY_SCALE * jax.nn.sigmoid(lo[:, rw:2 * rw])).T
    af_out[0] = jax.nn.sigmoid(lo[:, 2 * rw:3 * rw]).T
    ab_out[0] = jax.nn.sigmoid(lo[:, 3 * rw:]).T
    dg = ps[:, 3 * rw + DECAY_LORA + ICLR_LORA:]
    g_out[0] = _bdot(jax.nn.sigmoid(dg), gup_ref[...]).T


def _in_projection(x, sh1, sc1, g_mix, w_in, seg, cos_t, sin_t, q_norm, k_norm, mu_shift,
                   wlo, b0, g_up, *, ts):
    B, S, D = x.shape
    n_in = w_in.shape[1]
    rw = g_up.shape[1]
    kv_w = ATT_KV_HEADS * HEAD_DIM
    att_w = n_in - 2 * kv_w - (3 * rw + DECAY_LORA + ICLR_LORA + GATE_LORA)
    n_t = S // ts
    hb = ts // SUBLANES
    n_hb = S // SUBLANES
    const = lambda b, i: (0, 0)
    tile = lambda b, i: (b, i, 0)
    modspec = pl.BlockSpec((1, 1, D), lambda b, i: (b, 0, 0))
    in_specs = [
        pl.BlockSpec((1, ts, D), tile),
        pl.BlockSpec((1, SUBLANES, D), lambda b, i: (b, jnp.maximum(i * hb - 1, 0), 0)),
        pl.BlockSpec((1, SUBLANES, D), lambda b, i: (b, jnp.minimum((i + 1) * hb, n_hb - 1), 0)),
        modspec, modspec,
        pl.BlockSpec((1, D), const),
        pl.BlockSpec((D, n_in), const),
        pl.BlockSpec(seg.shape, const),
        pl.BlockSpec((ts, LANES), lambda b, i: (i, 0)),
        pl.BlockSpec((ts, LANES), lambda b, i: (i, 0)),
        pl.BlockSpec((1, att_w), const),
        pl.BlockSpec((1, kv_w), const),
        pl.BlockSpec((1, mu_shift.shape[1]), const),
        pl.BlockSpec(wlo.shape, const),
        pl.BlockSpec(b0.shape, const),
        pl.BlockSpec(g_up.shape, const),
    ]
    out_shapes = ([jax.ShapeDtypeStruct((B, S, att_w), BF16),
                   jax.ShapeDtypeStruct((B, S, kv_w), BF16),
                   jax.ShapeDtypeStruct((B, S, kv_w), BF16)]
                  + [jax.ShapeDtypeStruct((B, rw, S), F32)] * 8)
    out_specs = ([pl.BlockSpec((1, ts, att_w), tile),
                  pl.BlockSpec((1, ts, kv_w), tile),
                  pl.BlockSpec((1, ts, kv_w), tile)]
                 + [pl.BlockSpec((1, rw, ts), lambda b, i: (b, 0, i))] * 8)
    return pl.pallas_call(
        functools.partial(_inproj_kernel, att_w=att_w, kv_w=kv_w, rw=rw),
        grid=(B, n_t),
        in_specs=in_specs,
        out_specs=out_specs,
        out_shape=out_shapes,
        compiler_params=_cparams("parallel", "arbitrary"),
        name="in_projection",
    )(x, x, x, sh1, sc1, g_mix, w_in, seg, cos_t, sin_t, q_norm, k_norm, mu_shift, wlo, b0, g_up)


def _attn_kernel(q_ref, k_ref, v_ref, o_ref):
    k = k_ref[0, 0]
    v = v_ref[0, 0]
    grp = q_ref.shape[2] // HEAD_DIM
    outs = []
    for hh in range(grp):
        q = q_ref[0, :, hh * HEAD_DIM:(hh + 1) * HEAD_DIM]
        s = lax.dot_general(q, k, (((1,), (1,)), ((), ())), preferred_element_type=F32)
        e = jnp.exp2(s - jnp.max(s, axis=-1, keepdims=True)).astype(BF16)
        o = jnp.dot(e, v, preferred_element_type=F32)
        outs.append(o[:, :HEAD_DIM] / o[:, HEAD_DIM:HEAD_DIM + 1])
    o_ref[0] = jnp.concatenate(outs, axis=1).astype(o_ref.dtype)


def _attention(q, k, v, *, tq):
    B, S, att_w = q.shape
    gw = att_w // ATT_KV_HEADS
    k4 = k.reshape(B, S, ATT_KV_HEADS, HEAD_DIM).transpose(0, 2, 1, 3)
    v4 = v.reshape(B, S, ATT_KV_HEADS, HEAD_DIM).transpose(0, 2, 1, 3)
    v4 = jnp.concatenate([v4, jnp.ones_like(v4)], axis=-1)
    return pl.pallas_call(
        _attn_kernel,
        grid=(B, ATT_KV_HEADS, S // tq),
        in_specs=[pl.BlockSpec((1, tq, gw), lambda b, g, i: (b, i, g)),
                  pl.BlockSpec((1, 1, S, HEAD_DIM), lambda b, g, i: (b, g, 0, 0)),
                  pl.BlockSpec((1, 1, S, 2 * HEAD_DIM), lambda b, g, i: (b, g, 0, 0))],
        out_specs=pl.BlockSpec((1, tq, gw), lambda b, g, i: (b, i, g)),
        out_shape=jax.ShapeDtypeStruct((B, S, att_w), BF16),
        compiler_params=_cparams("parallel", "parallel", "arbitrary"),
        name="attention",
    )(q, k4, v4)


def _wkv_kernel(r_ref, k_ref, v_ref, w_ref, a_ref, kk_scale_ref, ka_ref, rk_ref,
                y_ref, bs_ref, s_ref, g_ref, ca_ref, cb_ref, cr_ref, cn_ref, *, reverse):
    c = pl.program_id(1)
    tc, n, _ = r_ref.shape

    @pl.when(c == 0)
    def _():
        s_ref[...] = jnp.zeros_like(s_ref)

    def tidx(j):
        return (tc - 1 - j) if reverse else j

    def prepare(j, g):
        t = tidx(j)
        kt = k_ref[t]
        a = a_ref[t]
        r = r_ref[t]
        kk = kt * kk_scale_ref[...]
        kk = kk / jnp.maximum(jnp.sqrt(jnp.sum(kk * kk, axis=0, keepdims=True)), L2_EPS)
        kd = kt * (1.0 + (a - 1.0) * ka_ref[...])
        bs_ref[t] = jnp.sum(r * kd * rk_ref[...], axis=0, keepdims=True)
        cn_ref[t] = -(kk * g)
        g = g * w_ref[t]
        g_inv = 1.0 / g
        ca_ref[t] = kk * a * g_inv
        cb_ref[t] = kd * g_inv
        cr_ref[t] = r * g
        return g

    g_ref[...] = lax.fori_loop(0, tc, prepare, jnp.ones((n, LANES), F32), unroll=4)

    t0 = tidx(0)
    sa0 = jnp.zeros((n, LANES), F32)
    for kx in range(n):
        sa0 = sa0 + s_ref[kx] * cn_ref[t0, kx:kx + 1, :]

    def step(j, sa):
        t = tidx(j)
        tn = tidx(jnp.minimum(j + 1, tc - 1))
        v_t = v_ref[t]
        y = jnp.zeros((n, LANES), F32)
        sa_next = jnp.zeros((n, LANES), F32)
        for kx in range(n):
            row = pl.ds(kx, 1)
            z_new = s_ref[kx] + sa * ca_ref[t, row, :] + v_t * cb_ref[t, row, :]
            s_ref[kx] = z_new
            y = y + z_new * cr_ref[t, row, :]
            sa_next = sa_next + z_new * cn_ref[tn, row, :]
        y_ref[t] = y
        return sa_next

    lax.fori_loop(0, tc, step, sa0)

    for kx in range(n):
        s_ref[kx] = s_ref[kx] * g_ref[kx:kx + 1, :]


def _wkv_scan(r, k, v, w, a, kk_scale, ka, rk, *, reverse, tc):
    S, N, BH = r.shape
    n_c = S // tc
    n_l = BH // LANES
    if reverse:
        tmap = lambda j, c: (n_c - 1 - c, 0, j)
    else:
        tmap = lambda j, c: (c, 0, j)
    blk = pl.BlockSpec((tc, N, LANES), tmap)
    par = pl.BlockSpec((N, LANES), lambda j, c: (0, j))
    return pl.pallas_call(
        functools.partial(_wkv_kernel, reverse=reverse),
        grid=(n_l, n_c),
        in_specs=[blk] * 5 + [par] * 3,
        out_specs=[blk, pl.BlockSpec((tc, 1, LANES), tmap)],
        out_shape=[jax.ShapeDtypeStruct((S, N, BH), F32),
                   jax.ShapeDtypeStruct((S, 1, BH), F32)],
        scratch_shapes=([pltpu.VMEM((N, N, LANES), F32), pltpu.VMEM((N, LANES), F32)]
                        + [pltpu.VMEM((tc, N, LANES), F32)] * 4),
        compiler_params=_cparams("parallel", "arbitrary"),
        name="wkv7_scan_bwd" if reverse else "wkv7_scan_fwd",
    )(r, k, v, w, a, kk_scale, ka, rk)


def _gn_kernel(yf_ref, yb_ref, bf_ref, bb_ref, v_ref, lw_ref, lb_ref, o_ref):
    y = yf_ref[...] + yb_ref[...]
    mean = jnp.mean(y, axis=1, keepdims=True)
    d = y - mean
    var = jnp.mean(d * d, axis=1, keepdims=True)
    o_ref[...] = (d * lax.rsqrt(var + GN_EPS) * lw_ref[...] + lb_ref[...]
                  + (bf_ref[...] + bb_ref[...]) * v_ref[...])


def _group_norm_bonus(yf, yb, bf, bb, v, lw, lb, *, tc):
    S, N, BH = yf.shape
    blk = pl.BlockSpec((tc, N, BH), lambda c: (c, 0, 0))
    sblk = pl.BlockSpec((tc, 1, BH), lambda c: (c, 0, 0))
    par = pl.BlockSpec((N, BH), lambda c: (0, 0))
    return pl.pallas_call(
        _gn_kernel,
        grid=(S // tc,),
        in_specs=[blk, blk, sblk, sblk, blk, par, par],
        out_specs=blk,
        out_shape=jax.ShapeDtypeStruct((S, N, BH), F32),
        compiler_params=_cparams("parallel"),
        name="wkv7_groupnorm",
    )(yf, yb, bf, bb, v, lw, lb)


def _outproj_kernel(oa_ref, yo_ref, g_ref, x_ref, gt_ref, wo_ref, gf_ref, sh_ref, sc_ref,
                    wr_ref, x1_out, h2_out, aff_out):
    yg = (yo_ref[0] * g_ref[0]).T.astype(BF16)
    mix = jnp.concatenate([oa_ref[0], yg], axis=1)
    x1 = x_ref[0] + gt_ref[0] * jnp.dot(mix, wo_ref[...], preferred_element_type=F32)
    x1_out[0] = x1
    h2 = _rms_rows(x1) * gf_ref[...] * (1.0 + sc_ref[0]) + sh_ref[0]
    h2_out[0] = h2.astype(BF16)
    wr = wr_ref[...]
    nt = (((1,), (1,)), ((), ()))
    w_hi = wr.astype(BF16)
    w_lo = (wr - w_hi.astype(F32)).astype(BF16)
    h_hi = h2.astype(BF16)
    h_lo = (h2 - h_hi.astype(F32)).astype(BF16)
    logits = (lax.dot_general(w_hi, h_hi, nt, preferred_element_type=F32)
              + lax.dot_general(w_hi, h_lo, nt, preferred_element_type=F32)
              + lax.dot_general(w_lo, h_hi, nt, preferred_element_type=F32))
    m = jnp.max(logits, axis=0, keepdims=True)
    e = jnp.exp(logits - m)
    aff_out[0] = e / jnp.sum(e, axis=0, keepdims=True)


def _out_projection(o_att, yo, g, x, gt1, w_out, g_ffn, sh2, sc2, w_router_t, *, ts):
    B, S, D = x.shape
    aw = o_att.shape[2]
    rw = yo.shape[1]
    E = w_router_t.shape[0]
    tile = lambda b, i: (b, i, 0)
    const = lambda b, i: (0, 0)
    modspec = pl.BlockSpec((1, 1, D), lambda b, i: (b, 0, 0))
    cmajor = pl.BlockSpec((1, rw, ts), lambda b, i: (b, 0, i))
    return pl.pallas_call(
        _outproj_kernel,
        grid=(B, S // ts),
        in_specs=[pl.BlockSpec((1, ts, aw), tile),
                  cmajor,
                  cmajor,
                  pl.BlockSpec((1, ts, D), tile),
                  modspec,
                  pl.BlockSpec((aw + rw, D), const),
                  pl.BlockSpec((1, D), const),
                  modspec, modspec,
                  pl.BlockSpec((E, D), const)],
        out_specs=[pl.BlockSpec((1, ts, D), tile),
                   pl.BlockSpec((1, ts, D), tile),
                   pl.BlockSpec((1, E, ts), lambda b, i: (b, 0, i))],
        out_shape=[jax.ShapeDtypeStruct((B, S, D), F32),
                   jax.ShapeDtypeStruct((B, S, D), BF16),
                   jax.ShapeDtypeStruct((B, E, S), F32)],
        compiler_params=_cparams("parallel", "arbitrary"),
        name="out_projection",
    )(o_att, yo, g, x, gt1, w_out, g_ffn, sh2, sc2, w_router_t)


def _route_kernel(aff_ref, tri_ref, wgt_out, pos_out, *, cap):
    aff = aff_ref[...]
    capf = jnp.float32(cap)

    def count(mask):
        return jnp.sum(jnp.where(mask, 1.0, 0.0), axis=1, keepdims=True)

    def as_float(word):
        return pltpu.bitcast(word, F32)

    def bit_step(i, thr):
        cand = thr | jnp.left_shift(jnp.int32(1), 30 - i)
        return jnp.where(count(aff >= as_float(cand)) >= capf, cand, thr)

    thr = lax.fori_loop(0, 31, bit_step, jnp.zeros((aff.shape[0], 1), jnp.int32))
    thr_f = as_float(thr)
    gt = aff > thr_f
    eq = aff == thr_f
    need = capf - count(gt)
    tri = tri_ref[...]
    eq_rank = jnp.dot(jnp.where(eq, 1.0, 0.0).astype(BF16), tri, preferred_element_type=F32)
    sel = jnp.where(gt, 1.0, jnp.where(eq, jnp.where(eq_rank <= need, 1.0, 0.0), 0.0))
    rank = jnp.dot(sel.astype(BF16), tri, preferred_element_type=F32)
    chosen = sel > 0.5
    wgt_out[...] = jnp.where(chosen, aff, 0.0)
    pos_out[...] = jnp.where(chosen, rank - 1.0, -1.0).astype(jnp.int32)


def _routing(aff, tri, *, cap):
    B, E, S = aff.shape
    rows = _pick_tile(B * E, LANES)
    blk = pl.BlockSpec((rows, S), lambda i: (i, 0))
    wgt, pos = pl.pallas_call(
        functools.partial(_route_kernel, cap=cap),
        grid=(B * E // rows,),
        in_specs=[blk, pl.BlockSpec((S, S), lambda i: (0, 0))],
        out_specs=[blk, blk],
        out_shape=[jax.ShapeDtypeStruct((B * E, S), F32),
                   jax.ShapeDtypeStruct((B * E, S), jnp.int32)],
        compiler_params=_cparams("parallel"),
        name="ec_routing",
    )(aff.reshape(B * E, S), tri)
    return wgt.reshape(B, E, S), pos.reshape(B, E, S)


def _expert_kernel(h_ref, pos_ref, wgt_ref, wg_ref, wu_ref, wd_ref, x1_ref, gt_ref, acc_ref,
                   *, cap):
    e = pl.program_id(1)

    @pl.when(e == 0)
    def _():
        acc_ref[...] = jnp.zeros_like(acc_ref)

    slab = x1_ref.shape[1]
    rows = pl.ds(pl.multiple_of(e * slab, slab), slab)
    acc_ref[0, rows, :] += x1_ref[0]

    pos = pos_ref[0, 0]
    wgt = wgt_ref[0, 0]
    s = pos.shape[1]
    slot = lax.broadcasted_iota(jnp.int32, (cap, s), 0)
    hit = pos == slot
    onehot = jnp.where(hit, 1.0, 0.0).astype(BF16)
    vals = jnp.sum(jnp.where(hit, wgt, 0.0), axis=1, keepdims=True)
    hg = jnp.dot(onehot, h_ref[0], preferred_element_type=F32).astype(BF16)
    gate = jnp.dot(hg, wg_ref[0], preferred_element_type=F32)
    up = jnp.dot(hg, wu_ref[0], preferred_element_type=F32)
    hid = (gate * jax.nn.sigmoid(gate) * up).astype(BF16)
    y = jnp.dot(hid, wd_ref[0], preferred_element_type=F32) * vals * gt_ref[0]
    acc_ref[0] += lax.dot_general(onehot, y.astype(BF16), (((0,), (0,)), ((), ())),
                                  preferred_element_type=F32)


def _experts(h2, pos, wgt, w_gate, w_up, w_down, x1, gt2, *, cap):
    B, S, D = h2.shape
    E, _, F = w_gate.shape
    row = pl.BlockSpec((1, 1, 1, S), lambda b, e: (b, e, 0, 0))
    return pl.pallas_call(
        functools.partial(_expert_kernel, cap=cap),
        grid=(B, E),
        in_specs=[pl.BlockSpec((1, S, D), lambda b, e: (b, 0, 0)),
                  row, row,
                  pl.BlockSpec((1, D, F), lambda b, e: (e, 0, 0)),
                  pl.BlockSpec((1, D, F), lambda b, e: (e, 0, 0)),
                  pl.BlockSpec((1, F, D), lambda b, e: (e, 0, 0)),
                  pl.BlockSpec((1, S // E, D), lambda b, e: (b, e, 0)),
                  pl.BlockSpec((1, 1, D), lambda b, e: (b, 0, 0))],
        out_specs=pl.BlockSpec((1, S, D), lambda b, e: (b, 0, 0)),
        out_shape=jax.ShapeDtypeStruct((B, S, D), F32),
        compiler_params=_cparams("parallel", "arbitrary"),
        name="ec_experts",
    )(h2, pos.reshape(B, E, 1, S), wgt.reshape(B, E, 1, S), w_gate, w_up, w_down, x1, gt2)


def _rope_tables(seq):
    rows = seq // GRID_W
    row = jnp.repeat(jnp.arange(rows, dtype=F32), GRID_W)
    col = jnp.tile(jnp.arange(GRID_W, dtype=F32), rows)
    n_pairs_axis = HEAD_DIM // 4
    freqs = ROPE_THETA ** (-jnp.arange(n_pairs_axis, dtype=F32) / n_pairs_axis)
    ang = jnp.concatenate([row[:, None] * freqs, col[:, None] * freqs], axis=-1)
    cos = jnp.repeat(jnp.cos(ang), 2, axis=1)
    sin = jnp.repeat(jnp.sin(ang), 2, axis=1) * jnp.tile(jnp.array([-1.0, 1.0], F32), HEAD_DIM // 2)
    reps = LANES // HEAD_DIM
    return jnp.tile(cos, (1, reps)), jnp.tile(sin, (1, reps))


def _to_scan_layout(z, bh_pad):
    B, R, S = z.shape
    H = R // RWKV_HEAD
    zt = z.reshape(B, H, RWKV_HEAD, S).transpose(3, 2, 0, 1).reshape(S, RWKV_HEAD, B * H)
    if bh_pad != B * H:
        zt = jnp.pad(zt, ((0, 0), (0, 0), (0, bh_pad - B * H)))
    return zt


def _lane_param(p_hn, B, bh_pad):
    H, N = p_hn.shape
    t = jnp.tile(p_hn.T[:, None, :], (1, B, 1)).reshape(N, B * H)
    if bh_pad != B * H:
        t = jnp.pad(t, ((0, 0), (0, bh_pad - B * H)))
    return t


def _pick_tile(n, pref):
    t = min(n, pref)
    while n % t:
        t //= 2
    return t


def kernel(x, c, w_ada, b_ada, g_mix, w_in, q_norm, k_norm, mu_shift, w0, w_up, a0, a_up, g_up,
           k_k, k_a, r_k, ln_w, ln_b, w_out, g_ffn, w_router, w_gate, w_up_e, w_down):
    B, S, D = x.shape
    depth = w_ada.shape[0]
    rw = g_up.shape[2]
    H = rw // RWKV_HEAD
    cap = EC_CAPACITY * S // N_EXPERTS
    ts = _pick_tile(S, 512)
    tq = _pick_tile(S, 512)
    tc = _pick_tile(S, 64)
    bh = B * H
    bh_pad = -(-bh // LANES) * LANES

    cos_t, sin_t = _rope_tables(S)
    seg_n = max(w_in.shape[2] - 2 * ATT_KV_HEADS * HEAD_DIM
                - (3 * rw + DECAY_LORA + ICLR_LORA + GATE_LORA), ATT_KV_HEADS * HEAD_DIM)
    seg_id = np.arange(seg_n) // HEAD_DIM
    seg = jnp.asarray((seg_id[:, None] == seg_id[None, :]).astype(np.float32) / HEAD_DIM, BF16)
    tri = jnp.asarray(np.arange(S)[:, None] <= np.arange(S)[None, :], BF16)

    for l in range(depth):
        mod = _modulation(c, w_ada[l].astype(BF16), b_ada[l])
        sh1, sc1, gt1, sh2, sc2, gt2 = [m.reshape(B, 1, D) for m in jnp.split(mod, 6, axis=-1)]

        zeros = jnp.zeros((DECAY_LORA, rw), F32)
        wlo = jnp.concatenate([
            jnp.concatenate([w_up[l, 0], w_up[l, 1], zeros, zeros], axis=1),
            jnp.concatenate([zeros, zeros, a_up[l, 0], a_up[l, 1]], axis=1)], axis=0).astype(BF16)
        b0 = jnp.concatenate([w0[l, 0], w0[l, 1], a0[l, 0], a0[l, 1]]).reshape(1, 4 * rw)
        att_w = seg_n
        qg = jnp.tile(q_norm[l], att_w // HEAD_DIM).reshape(1, att_w)
        kg = jnp.tile(k_norm[l], ATT_KV_HEADS).reshape(1, ATT_KV_HEADS * HEAD_DIM)

        (q, k, v, r, kr, vr, wf, wb, af, ab, g) = _in_projection(
            x, sh1, sc1, g_mix[l].reshape(1, D), w_in[l].astype(BF16), seg, cos_t, sin_t, qg, kg,
            mu_shift[l].reshape(1, -1), wlo, b0, g_up[l].astype(BF16), ts=ts)

        o_att = _attention(q, k, v, tq=tq)

        rt, kt, vt, wft, wbt, aft, abt = [_to_scan_layout(z, bh_pad)
                                          for z in (r, kr, vr, wf, wb, af, ab)]
        kk_scale = _lane_param(k_k[l].reshape(H, RWKV_HEAD), B, bh_pad)
        ka = _lane_param(k_a[l].reshape(H, RWKV_HEAD), B, bh_pad)
        rk = _lane_param(r_k[l], B, bh_pad)
        yf, bf = _wkv_scan(rt, kt, vt, wft, aft, kk_scale, ka, rk, reverse=False, tc=tc)
        yb, bb = _wkv_scan(rt, kt, vt, wbt, abt, kk_scale, ka, rk, reverse=True, tc=tc)
        lw = _lane_param(ln_w[l].reshape(H, RWKV_HEAD), B, bh_pad)
        lb = _lane_param(ln_b[l].reshape(H, RWKV_HEAD), B, bh_pad)
        yo_t = _group_norm_bonus(yf, yb, bf, bb, vt, lw, lb, tc=tc)
        yo = (yo_t[:, :, :bh].reshape(S, RWKV_HEAD, B, H).transpose(2, 3, 1, 0)
              .reshape(B, rw, S))

        x1, h2, aff = _out_projection(o_att, yo, g, x, gt1, w_out[l].astype(BF16),
                                      g_ffn[l].reshape(1, D), sh2, sc2, w_router[l].T,
                                      ts=_pick_tile(S, 512))
        wgt, pos = _routing(aff, tri, cap=cap)
        x = _experts(h2, pos, wgt, w_gate[l].astype(BF16), w_up_e[l].astype(BF16),
                     w_down[l].astype(BF16), x1, gt2, cap=cap)
    return x
```

```python
import functools

import jax
import jax.numpy as jnp
import numpy as np
from jax import lax
from jax.experimental import pallas as pl
from jax.experimental.pallas import tpu as pltpu

LANES = 128
SUBLANES = 8
VMEM_LIMIT_BYTES = 56 * 1024 * 1024

GRID_W = 64
ROPE_THETA = 10000.0
HEAD_DIM = 64
ATT_KV_HEADS = 2
RWKV_HEAD = 64
DECAY_LORA = 64
ICLR_LORA = 64
GATE_LORA = 128
DECAY_SCALE = 0.606531
GN_EPS = 64e-5
NORM_EPS = 1e-6
L2_EPS = 1e-12
N_EXPERTS = 16
EC_CAPACITY = 2
LOG2_E = 1.4426950408889634

F32 = jnp.float32
BF16 = jnp.bfloat16


def _cparams(*sem):
    return pltpu.CompilerParams(dimension_semantics=sem, vmem_limit_bytes=VMEM_LIMIT_BYTES)


def _bdot(a, b):
    return jnp.dot(a.astype(BF16), b.astype(BF16), preferred_element_type=F32)


def _split_dot(a, b_bf16):
    hi = a.astype(BF16)
    lo = (a - hi.astype(F32)).astype(BF16)
    return (jnp.dot(hi, b_bf16, preferred_element_type=F32)
            + jnp.dot(lo, b_bf16, preferred_element_type=F32))


def _rms_rows(x):
    return x * lax.rsqrt(jnp.mean(x * x, axis=-1, keepdims=True) + NORM_EPS)


def _mod_kernel(c_ref, w_ref, b_ref, o_ref):
    c = c_ref[...]
    cond = c * jax.nn.sigmoid(c)
    o_ref[...] = _bdot(cond, w_ref[...]) + b_ref[...]


def _modulation(c, w_ada, b_ada):
    B, D = c.shape
    n_out = w_ada.shape[1]
    tn = D
    return pl.pallas_call(
        _mod_kernel,
        grid=(n_out // tn,),
        in_specs=[pl.BlockSpec((B, D), lambda j: (0, 0)),
                  pl.BlockSpec((D, tn), lambda j: (0, j)),
                  pl.BlockSpec((1, tn), lambda j: (0, j))],
        out_specs=pl.BlockSpec((B, tn), lambda j: (0, j)),
        out_shape=jax.ShapeDtypeStruct((B, n_out), F32),
        compiler_params=_cparams("arbitrary"),
        name="adaln_mod",
    )(c, w_ada, b_ada.reshape(1, n_out))


def _rope(x, cos, sin_signed):
    n = x.shape[-1]
    lane = lax.broadcasted_iota(jnp.int32, x.shape, 1)
    nxt = pltpu.roll(x, n - 1, axis=1)
    prv = pltpu.roll(x, 1, axis=1)
    swapped = jnp.where(lane % 2 == 0, nxt, prv)
    return x * cos + swapped * sin_signed


def _inproj_kernel(x_ref, xp_ref, xn_ref, sh_ref, sc_ref, gm_ref, w_ref, seg_ref, cos_ref,
                   sin_ref, qg_ref, kg_ref, mu_ref, wlo_ref, b0_ref, gup_ref,
                   q_out, k_out, v_out, r_out, kr_out, vr_out, wf_out, wb_out, af_out,
                   ab_out, g_out, *, att_w, kv_w, rw):
    i = pl.program_id(1)
    n_i = pl.num_programs(1)
    ts = x_ref.shape[1]
    gain = gm_ref[...]
    scale = 1.0 + sc_ref[0]
    shift = sh_ref[0]

    def norm_mod(xv):
        return (_rms_rows(xv) * gain * scale + shift).astype(BF16)

    halo = jnp.concatenate([xp_ref[0], xn_ref[0]], axis=0)
    h = jnp.concatenate([norm_mod(x_ref[0]), norm_mod(halo)], axis=0)
    p_all = jnp.dot(h, w_ref[...], preferred_element_type=F32)
    p = p_all[:ts]

    seg = seg_ref[...]

    def qk_norm_rope(z, gain_row, reps):
        ms = _split_dot(z * z, seg[: z.shape[1], : z.shape[1]])
        zn = z * lax.rsqrt(ms + NORM_EPS) * gain_row
        cos = jnp.concatenate([cos_ref[...]] * reps, axis=1) if reps > 1 else cos_ref[...]
        sin = jnp.concatenate([sin_ref[...]] * reps, axis=1) if reps > 1 else sin_ref[...]
        return _rope(zn, cos, sin)

    q = qk_norm_rope(p[:, :att_w], qg_ref[...], att_w // LANES)
    q_out[0] = (q * (HEAD_DIM ** -0.5 * LOG2_E)).astype(BF16)
    k = qk_norm_rope(p[:, att_w:att_w + kv_w], kg_ref[...], kv_w // LANES)
    k_out[0] = k.astype(BF16)
    v_out[0] = p[:, att_w + kv_w:att_w + 2 * kv_w].astype(BF16)

    off = att_w + 2 * kv_w
    pr = p[:, off:]
    ph = p_all[ts:, off:]
    prev_row = jnp.where(i == 0, 0.0, ph[SUBLANES - 1:SUBLANES, :])
    next_row = jnp.where(i == n_i - 1, 0.0, ph[SUBLANES:SUBLANES + 1, :])
    row = lax.broadcasted_iota(jnp.int32, pr.shape, 0)
    prev = jnp.where(row == 0, prev_row, pltpu.roll(pr, 1, axis=0))
    nxt = jnp.where(row == ts - 1, next_row, pltpu.roll(pr, ts - 1, axis=0))
    ps = pr + mu_ref[...] * (0.5 * (prev + nxt) - pr)

    r_out[0] = ps[:, :rw].T
    kr_out[0] = ps[:, rw:2 * rw].T
    vr_out[0] = ps[:, 2 * rw:3 * rw].T
    lo_in = ps[:, 3 * rw:3 * rw + DECAY_LORA + ICLR_LORA]
    lane = lax.broadcasted_iota(jnp.int32, lo_in.shape, 1)
    lo_in = jnp.where(lane < DECAY_LORA, jnp.tanh(lo_in), lo_in)
    lo = _bdot(lo_in, wlo_ref[...]) + b0_ref[...]
    wf_out[0] = jnp.exp(-DECAY_SCALE * jax.nn.sigmoid(lo[:, :rw])).T
    wb_out[0] = jnp.exp(-DECAY_SCALE * jax.nn.sigmoid(lo[:, rw:2 * rw])).T
    af_out[0] = jax.nn.sigmoid(lo[:, 2 * rw:3 * rw]).T
    ab_out[0] = jax.nn.sigmoid(lo[:, 3 * rw:]).T
    dg = ps[:, 3 * rw + DECAY_LORA + ICLR_LORA:]
    g_out[0] = _bdot(jax.nn.sigmoid(dg), gup_ref[...]).T


def _in_projection(x, sh1, sc1, g_mix, w_in, seg, cos_t, sin_t, q_norm, k_norm, mu_shift,
                   wlo, b0, g_up, *, ts):
    B, S, D = x.shape
    n_in = w_in.shape[1]
    rw = g_up.shape[1]
    kv_w = ATT_KV_HEADS * HEAD_DIM
    att_w = n_in - 2 * kv_w - (3 * rw + DECAY_LORA + ICLR_LORA + GATE_LORA)
    n_t = S // ts
    hb = ts // SUBLANES
    n_hb = S // SUBLANES
    const = lambda b, i: (0, 0)
    tile = lambda b, i: (b, i, 0)
    modspec = pl.BlockSpec((1, 1, D), lambda b, i: (b, 0, 0))
    in_specs = [
        pl.BlockSpec((1, ts, D), tile),
        pl.BlockSpec((1, SUBLANES, D), lambda b, i: (b, jnp.maximum(i * hb - 1, 0), 0)),
        pl.BlockSpec((1, SUBLANES, D), lambda b, i: (b, jnp.minimum((i + 1) * hb, n_hb - 1), 0)),
        modspec, modspec,
        pl.BlockSpec((1, D), const),
        pl.BlockSpec((D, n_in), const),
        pl.BlockSpec(seg.shape, const),
        pl.BlockSpec((ts, LANES), lambda b, i: (i, 0)),
        pl.BlockSpec((ts, LANES), lambda b, i: (i, 0)),
        pl.BlockSpec((1, att_w), const),
        pl.BlockSpec((1, kv_w), const),
        pl.BlockSpec((1, mu_shift.shape[1]), const),
        pl.BlockSpec(wlo.shape, const),
        pl.BlockSpec(b0.shape, const),
        pl.BlockSpec(g_up.shape, const),
    ]
    out_shapes = ([jax.ShapeDtypeStruct((B, S, att_w), BF16),
                   jax.ShapeDtypeStruct((B, S, kv_w), BF16),
                   jax.ShapeDtypeStruct((B, S, kv_w), BF16)]
                  + [jax.ShapeDtypeStruct((B, rw, S), F32)] * 8)
    out_specs = ([pl.BlockSpec((1, ts, att_w), tile),
                  pl.BlockSpec((1, ts, kv_w), tile),
                  pl.BlockSpec((1, ts, kv_w), tile)]
                 + [pl.BlockSpec((1, rw, ts), lambda b, i: (b, 0, i))] * 8)
    return pl.pallas_call(
        functools.partial(_inproj_kernel, att_w=att_w, kv_w=kv_w, rw=rw),
        grid=(B, n_t),
        in_specs=in_specs,
        out_specs=out_specs,
        out_shape=out_shapes,
        compiler_params=_cparams("parallel", "arbitrary"),
        name="in_projection",
    )(x, x, x, sh1, sc1, g_mix, w_in, seg, cos_t, sin_t, q_norm, k_norm, mu_shift, wlo, b0, g_up)


def _attn_kernel(q_ref, k_ref, v_ref, o_ref):
    n_kv = k_ref.shape[2] // HEAD_DIM
    grp = q_ref.shape[2] // (n_kv * HEAD_DIM)
    ones = jnp.ones((v_ref.shape[1], HEAD_DIM), BF16)
    outs = []
    for g in range(n_kv):
        k = k_ref[0, :, g * HEAD_DIM:(g + 1) * HEAD_DIM]
        v = jnp.concatenate([v_ref[0, :, g * HEAD_DIM:(g + 1) * HEAD_DIM], ones], axis=1)
        for hh in range(g * grp, (g + 1) * grp):
            q = q_ref[0, :, hh * HEAD_DIM:(hh + 1) * HEAD_DIM]
            s = lax.dot_general(q, k, (((1,), (1,)), ((), ())), preferred_element_type=F32)
            e = jnp.exp2(s - jnp.max(s, axis=-1, keepdims=True)).astype(BF16)
            o = jnp.dot(e, v, preferred_element_type=F32)
            outs.append(o[:, :HEAD_DIM] / o[:, HEAD_DIM:HEAD_DIM + 1])
    o_ref[0] = jnp.concatenate(outs, axis=1).astype(o_ref.dtype)


def _attention(q, k, v, *, tq):
    B, S, att_w = q.shape
    kv_w = k.shape[2]
    tile = pl.BlockSpec((1, tq, att_w), lambda b, i: (b, i, 0))
    keys = pl.BlockSpec((1, S, kv_w), lambda b, i: (b, 0, 0))
    return pl.pallas_call(
        _attn_kernel,
        grid=(B, S // tq),
        in_specs=[tile, keys, keys],
        out_specs=tile,
        out_shape=jax.ShapeDtypeStruct((B, S, att_w), BF16),
        compiler_params=_cparams("parallel", "arbitrary"),
        name="attention",
    )(q, k, v)


def _wkv_kernel(*refs, reverse, merge):
    r_ref, k_ref, v_ref, w_ref, a_ref, kk_scale_ref, ka_ref, rk_ref = refs[:8]
    other = refs[8:12] if merge else ()
    y_ref, bs_ref, s_ref, g_ref, ca_ref, cb_ref, cr_ref, cn_ref = refs[8 + len(other):]
    c = pl.program_id(1)
    tc, n, _ = r_ref.shape

    @pl.when(c == 0)
    def _():
        s_ref[...] = jnp.zeros_like(s_ref)

    def tidx(j):
        return (tc - 1 - j) if reverse else j

    def prepare(j, g):
        t = tidx(j)
        kt = k_ref[t]
        a = a_ref[t]
        r = r_ref[t]
        kk = kt * kk_scale_ref[...]
        kk = kk / jnp.maximum(jnp.sqrt(jnp.sum(kk * kk, axis=0, keepdims=True)), L2_EPS)
        kd = kt * (1.0 + (a - 1.0) * ka_ref[...])
        bs_ref[t] = jnp.sum(r * kd * rk_ref[...], axis=0, keepdims=True)
        cn_ref[t] = -(kk * g)
        g = g * w_ref[t]
        g_inv = 1.0 / g
        ca_ref[t] = kk * a * g_inv
        cb_ref[t] = kd * g_inv
        cr_ref[t] = r * g
        return g

    g_ref[...] = lax.fori_loop(0, tc, prepare, jnp.ones((n, LANES), F32), unroll=4)

    t0 = tidx(0)
    sa0 = jnp.zeros((n, LANES), F32)
    for kx in range(n):
        sa0 = sa0 + s_ref[kx] * cn_ref[t0, kx:kx + 1, :]

    def step(j, sa):
        t = tidx(j)
        tn = tidx(jnp.minimum(j + 1, tc - 1))
        v_t = v_ref[t]
        y = jnp.zeros((n, LANES), F32)
        sa_next = jnp.zeros((n, LANES), F32)
        for kx in range(n):
            row = pl.ds(kx, 1)
            z_new = s_ref[kx] + sa * ca_ref[t, row, :] + v_t * cb_ref[t, row, :]
            s_ref[kx] = z_new
            y = y + z_new * cr_ref[t, row, :]
            sa_next = sa_next + z_new * cn_ref[tn, row, :]
        y_ref[t] = y
        return sa_next

    lax.fori_loop(0, tc, step, sa0)

    for kx in range(n):
        s_ref[kx] = s_ref[kx] * g_ref[kx:kx + 1, :]

    if merge:
        yo_ref, bo_ref, lw_ref, lb_ref = other

        def finish(t, carry):
            y = y_ref[t] + yo_ref[t]
            d = y - jnp.mean(y, axis=0, keepdims=True)
            var = jnp.mean(d * d, axis=0, keepdims=True)
            y_ref[t] = (d * lax.rsqrt(var + GN_EPS) * lw_ref[...] + lb_ref[...]
                        + (bs_ref[t] + bo_ref[t]) * v_ref[t])
            return carry

        lax.fori_loop(0, tc, finish, 0, unroll=4)


def _wkv_scan(r, k, v, w, a, kk_scale, ka, rk, other=None, *, reverse, tc):
    S, N, BH = r.shape
    n_c = S // tc
    n_l = BH // LANES
    if reverse:
        tmap = lambda j, c: (n_c - 1 - c, 0, j)
    else:
        tmap = lambda j, c: (c, 0, j)
    blk = pl.BlockSpec((tc, N, LANES), tmap)
    sblk = pl.BlockSpec((tc, 1, LANES), tmap)
    par = pl.BlockSpec((N, LANES), lambda j, c: (0, j))
    merge = other is not None
    extra_specs = [blk, sblk, par, par] if merge else []
    return pl.pallas_call(
        functools.partial(_wkv_kernel, reverse=reverse, merge=merge),
        grid=(n_l, n_c),
        in_specs=[blk] * 5 + [par] * 3 + extra_specs,
        out_specs=[blk, sblk],
        out_shape=[jax.ShapeDtypeStruct((S, N, BH), F32),
                   jax.ShapeDtypeStruct((S, 1, BH), F32)],
        scratch_shapes=([pltpu.VMEM((N, N, LANES), F32), pltpu.VMEM((N, LANES), F32)]
                        + [pltpu.VMEM((tc, N, LANES), F32)] * 4),
        compiler_params=_cparams("parallel", "arbitrary"),
        name="wkv7_scan_bwd" if reverse else "wkv7_scan_fwd",
    )(r, k, v, w, a, kk_scale, ka, rk, *(other or ()))


def _outproj_kernel(oa_ref, yo_ref, g_ref, x_ref, gt_ref, wo_ref, gf_ref, sh_ref, sc_ref,
                    wr_ref, x1_out, h2_out, aff_out):
    yg = (yo_ref[0] * g_ref[0]).T.astype(BF16)
    mix = jnp.concatenate([oa_ref[0], yg], axis=1)
    x1 = x_ref[0] + gt_ref[0] * jnp.dot(mix, wo_ref[...], preferred_element_type=F32)
    x1_out[0] = x1
    h2 = _rms_rows(x1) * gf_ref[...] * (1.0 + sc_ref[0]) + sh_ref[0]
    h2_out[0] = h2.astype(BF16)
    wr = wr_ref[...]
    nt = (((1,), (1,)), ((), ()))
    w_hi = wr.astype(BF16)
    w_lo = (wr - w_hi.astype(F32)).astype(BF16)
    h_hi = h2.astype(BF16)
    h_lo = (h2 - h_hi.astype(F32)).astype(BF16)
    logits = (lax.dot_general(w_hi, h_hi, nt, preferred_element_type=F32)
              + lax.dot_general(w_hi, h_lo, nt, preferred_element_type=F32)
              + lax.dot_general(w_lo, h_hi, nt, preferred_element_type=F32))
    m = jnp.max(logits, axis=0, keepdims=True)
    e = jnp.exp(logits - m)
    aff_out[0] = e / jnp.sum(e, axis=0, keepdims=True)


def _out_projection(o_att, yo, g, x, gt1, w_out, g_ffn, sh2, sc2, w_router_t, *, ts):
    B, S, D = x.shape
    aw = o_att.shape[2]
    rw = yo.shape[1]
    E = w_router_t.shape[0]
    tile = lambda b, i: (b, i, 0)
    const = lambda b, i: (0, 0)
    modspec = pl.BlockSpec((1, 1, D), lambda b, i: (b, 0, 0))
    cmajor = pl.BlockSpec((1, rw, ts), lambda b, i: (b, 0, i))
    return pl.pallas_call(
        _outproj_kernel,
        grid=(B, S // ts),
        in_specs=[pl.BlockSpec((1, ts, aw), tile),
                  cmajor,
                  cmajor,
                  pl.BlockSpec((1, ts, D), tile),
                  modspec,
                  pl.BlockSpec((aw + rw, D), const),
                  pl.BlockSpec((1, D), const),
                  modspec, modspec,
                  pl.BlockSpec((E, D), const)],
        out_specs=[pl.BlockSpec((1, ts, D), tile),
                   pl.BlockSpec((1, ts, D), tile),
                   pl.BlockSpec((1, E, ts), lambda b, i: (b, 0, i))],
        out_shape=[jax.ShapeDtypeStruct((B, S, D), F32),
                   jax.ShapeDtypeStruct((B, S, D), BF16),
                   jax.ShapeDtypeStruct((B, E, S), F32)],
        compiler_params=_cparams("parallel", "arbitrary"),
        name="out_projection",
    )(o_att, yo, g, x, gt1, w_out, g_ffn, sh2, sc2, w_router_t)


def _route_kernel(aff_ref, tri_ref, wgt_out, pos_out, *, cap):
    aff = aff_ref[...]
    capf = jnp.float32(cap)

    def count(mask):
        return jnp.sum(jnp.where(mask, 1.0, 0.0), axis=1, keepdims=True)

    def as_float(word):
        return pltpu.bitcast(word, F32)

    def bit_step(i, thr):
        cand = thr | jnp.left_shift(jnp.int32(1), 30 - i)
        return jnp.where(count(aff >= as_float(cand)) >= capf, cand, thr)

    thr = lax.fori_loop(0, 31, bit_step, jnp.zeros((aff.shape[0], 1), jnp.int32))
    thr_f = as_float(thr)
    gt = aff > thr_f
    eq = aff == thr_f
    need = capf - count(gt)
    tri = tri_ref[...]
    eq_rank = jnp.dot(jnp.where(eq, 1.0, 0.0).astype(BF16), tri, preferred_element_type=F32)
    sel = jnp.where(gt, 1.0, jnp.where(eq, jnp.where(eq_rank <= need, 1.0, 0.0), 0.0))
    rank = jnp.dot(sel.astype(BF16), tri, preferred_element_type=F32)
    chosen = sel > 0.5
    wgt_out[...] = jnp.where(chosen, aff, 0.0)
    pos_out[...] = jnp.where(chosen, rank - 1.0, -1.0).astype(jnp.int32)


def _routing(aff, tri, *, cap):
    B, E, S = aff.shape
    rows = _pick_tile(B * E, LANES)
    blk = pl.BlockSpec((rows, S), lambda i: (i, 0))
    wgt, pos = pl.pallas_call(
        functools.partial(_route_kernel, cap=cap),
        grid=(B * E // rows,),
        in_specs=[blk, pl.BlockSpec((S, S), lambda i: (0, 0))],
        out_specs=[blk, blk],
        out_shape=[jax.ShapeDtypeStruct((B * E, S), F32),
                   jax.ShapeDtypeStruct((B * E, S), jnp.int32)],
        compiler_params=_cparams("parallel"),
        name="ec_routing",
    )(aff.reshape(B * E, S), tri)
    return wgt.reshape(B, E, S), pos.reshape(B, E, S)


def _expert_kernel(h_ref, pos_ref, wgt_ref, wg_ref, wu_ref, wd_ref, x1_ref, gt_ref, acc_ref,
                   *, cap):
    e = pl.program_id(1)

    @pl.when(e == 0)
    def _():
        acc_ref[...] = jnp.zeros_like(acc_ref)

    slab = x1_ref.shape[1]
    rows = pl.ds(pl.multiple_of(e * slab, slab), slab)
    acc_ref[0, rows, :] += x1_ref[0]

    pos = pos_ref[0, 0]
    wgt = wgt_ref[0, 0]
    s = pos.shape[1]
    slot = lax.broadcasted_iota(jnp.int32, (cap, s), 0)
    hit = pos == slot
    onehot = jnp.where(hit, 1.0, 0.0).astype(BF16)
    vals = jnp.sum(jnp.where(hit, wgt, 0.0), axis=1, keepdims=True)
    hg = jnp.dot(onehot, h_ref[0], preferred_element_type=F32).astype(BF16)
    gate = jnp.dot(hg, wg_ref[0], preferred_element_type=F32)
    up = jnp.dot(hg, wu_ref[0], preferred_element_type=F32)
    hid = (gate * jax.nn.sigmoid(gate) * up).astype(BF16)
    y = jnp.dot(hid, wd_ref[0], preferred_element_type=F32) * vals * gt_ref[0]
    acc_ref[0] += lax.dot_general(onehot, y.astype(BF16), (((0,), (0,)), ((), ())),
                                  preferred_element_type=F32)


def _experts(h2, pos, wgt, w_gate, w_up, w_down, x1, gt2, *, cap):
    B, S, D = h2.shape
    E, _, F = w_gate.shape
    row = pl.BlockSpec((1, 1, 1, S), lambda b, e: (b, e, 0, 0))
    return pl.pallas_call(
        functools.partial(_expert_kernel, cap=cap),
        grid=(B, E),
        in_specs=[pl.BlockSpec((1, S, D), lambda b, e: (b, 0, 0)),
                  row, row,
                  pl.BlockSpec((1, D, F), lambda b, e: (e, 0, 0)),
                  pl.BlockSpec((1, D, F), lambda b, e: (e, 0, 0)),
                  pl.BlockSpec((1, F, D), lambda b, e: (e, 0, 0)),
                  pl.BlockSpec((1, S // E, D), lambda b, e: (b, e, 0)),
                  pl.BlockSpec((1, 1, D), lambda b, e: (b, 0, 0))],
        out_specs=pl.BlockSpec((1, S, D), lambda b, e: (b, 0, 0)),
        out_shape=jax.ShapeDtypeStruct((B, S, D), F32),
        compiler_params=_cparams("parallel", "arbitrary"),
        name="ec_experts",
    )(h2, pos.reshape(B, E, 1, S), wgt.reshape(B, E, 1, S), w_gate, w_up, w_down, x1, gt2)


def _rope_tables(seq):
    rows = seq // GRID_W
    row = jnp.repeat(jnp.arange(rows, dtype=F32), GRID_W)
    col = jnp.tile(jnp.arange(GRID_W, dtype=F32), rows)
    n_pairs_axis = HEAD_DIM // 4
    freqs = ROPE_THETA ** (-jnp.arange(n_pairs_axis, dtype=F32) / n_pairs_axis)
    ang = jnp.concatenate([row[:, None] * freqs, col[:, None] * freqs], axis=-1)
    cos = jnp.repeat(jnp.cos(ang), 2, axis=1)
    sin = jnp.repeat(jnp.sin(ang), 2, axis=1) * jnp.tile(jnp.array([-1.0, 1.0], F32), HEAD_DIM // 2)
    reps = LANES // HEAD_DIM
    return jnp.tile(cos, (1, reps)), jnp.tile(sin, (1, reps))


def _to_scan_layout(z, bh_pad):
    B, R, S = z.shape
    H = R // RWKV_HEAD
    zt = z.reshape(B, H, RWKV_HEAD, S).transpose(3, 2, 0, 1).reshape(S, RWKV_HEAD, B * H)
    if bh_pad != B * H:
        zt = jnp.pad(zt, ((0, 0), (0, 0), (0, bh_pad - B * H)))
    return zt


def _lane_param(p_hn, B, bh_pad):
    H, N = p_hn.shape
    t = jnp.tile(p_hn.T[:, None, :], (1, B, 1)).reshape(N, B * H)
    if bh_pad != B * H:
        t = jnp.pad(t, ((0, 0), (0, bh_pad - B * H)))
    return t


def _pick_tile(n, pref):
    t = min(n, pref)
    while n % t:
        t //= 2
    return t


def kernel(x, c, w_ada, b_ada, g_mix, w_in, q_norm, k_norm, mu_shift, w0, w_up, a0, a_up, g_up,
           k_k, k_a, r_k, ln_w, ln_b, w_out, g_ffn, w_router, w_gate, w_up_e, w_down):
    B, S, D = x.shape
    depth = w_ada.shape[0]
    rw = g_up.shape[2]
    H = rw // RWKV_HEAD
    cap = EC_CAPACITY * S // N_EXPERTS
    ts = _pick_tile(S, 512)
    tq = _pick_tile(S, 512)
    tc = _pick_tile(S, 64)
    bh = B * H
    bh_pad = -(-bh // LANES) * LANES

    cos_t, sin_t = _rope_tables(S)
    seg_n = max(w_in.shape[2] - 2 * ATT_KV_HEADS * HEAD_DIM
                - (3 * rw + DECAY_LORA + ICLR_LORA + GATE_LORA), ATT_KV_HEADS * HEAD_DIM)
    seg_id = np.arange(seg_n) // HEAD_DIM
    seg = jnp.asarray((seg_id[:, None] == seg_id[None, :]).astype(np.float32) / HEAD_DIM, BF16)
    tri = jnp.asarray(np.arange(S)[:, None] <= np.arange(S)[None, :], BF16)

    for l in range(depth):
        mod = _modulation(c, w_ada[l].astype(BF16), b_ada[l])
        sh1, sc1, gt1, sh2, sc2, gt2 = [m.reshape(B, 1, D) for m in jnp.split(mod, 6, axis=-1)]

        zeros = jnp.zeros((DECAY_LORA, rw), F32)
        wlo = jnp.concatenate([
            jnp.concatenate([w_up[l, 0], w_up[l, 1], zeros, zeros], axis=1),
            jnp.concatenate([zeros, zeros, a_up[l, 0], a_up[l, 1]], axis=1)], axis=0).astype(BF16)
        b0 = jnp.concatenate([w0[l, 0], w0[l, 1], a0[l, 0], a0[l, 1]]).reshape(1, 4 * rw)
        att_w = seg_n
        qg = jnp.tile(q_norm[l], att_w // HEAD_DIM).reshape(1, att_w)
        kg = jnp.tile(k_norm[l], ATT_KV_HEADS).reshape(1, ATT_KV_HEADS * HEAD_DIM)

        (q, k, v, r, kr, vr, wf, wb, af, ab, g) = _in_projection(
            x, sh1, sc1, g_mix[l].reshape(1, D), w_in[l].astype(BF16), seg, cos_t, sin_t, qg, kg,
            mu_shift[l].reshape(1, -1), wlo, b0, g_up[l].astype(BF16), ts=ts)

        o_att = _attention(q, k, v, tq=tq)

        rt, kt, vt, wft, wbt, aft, abt = [_to_scan_layout(z, bh_pad)
                                          for z in (r, kr, vr, wf, wb, af, ab)]
        kk_scale = _lane_param(k_k[l].reshape(H, RWKV_HEAD), B, bh_pad)
        ka = _lane_param(k_a[l].reshape(H, RWKV_HEAD), B, bh_pad)
        rk = _lane_param(r_k[l], B, bh_pad)
        lw = _lane_param(ln_w[l].reshape(H, RWKV_HEAD), B, bh_pad)
        lb = _lane_param(ln_b[l].reshape(H, RWKV_HEAD), B, bh_pad)
        yf, bf = _wkv_scan(rt, kt, vt, wft, aft, kk_scale, ka, rk, reverse=False, tc=tc)
        yo_t, _ = _wkv_scan(rt, kt, vt, wbt, abt, kk_scale, ka, rk, (yf, bf, lw, lb),
                            reverse=True, tc=tc)
        yo = (yo_t[:, :, :bh].reshape(S, RWKV_HEAD, B, H).transpose(2, 3, 1, 0)
              .reshape(B, rw, S))

        x1, h2, aff = _out_projection(o_att, yo, g, x, gt1, w_out[l].astype(BF16),
                                      g_ffn[l].reshape(1, D), sh2, sc2, w_router[l].T,
                                      ts=_pick_tile(S, 512))
        wgt, pos = _routing(aff, tri, cap=cap)
        x = _experts(h2, pos, wgt, w_gate[l].astype(BF16), w_up_e[l].astype(BF16),
                     w_down[l].astype(BF16), x1, gt2, cap=cap)
    return x
```

```python
import functools

import jax
import jax.numpy as jnp
import numpy as np
from jax import lax
from jax.experimental import pallas as pl
from jax.experimental.pallas import tpu as pltpu

LANES = 128
SUBLANES = 8
VMEM_LIMIT_BYTES = 56 * 1024 * 1024

GRID_W = 64
ROPE_THETA = 10000.0
HEAD_DIM = 64
ATT_KV_HEADS = 2
RWKV_HEAD = 64
DECAY_LORA = 64
ICLR_LORA = 64
GATE_LORA = 128
DECAY_SCALE = 0.606531
GN_EPS = 64e-5
NORM_EPS = 1e-6
L2_EPS = 1e-12
N_EXPERTS = 16
EC_CAPACITY = 2
LOG2_E = 1.4426950408889634

F32 = jnp.float32
BF16 = jnp.bfloat16


def _cparams(*sem):
    return pltpu.CompilerParams(dimension_semantics=sem, vmem_limit_bytes=VMEM_LIMIT_BYTES)


def _bdot(a, b):
    return jnp.dot(a.astype(BF16), b.astype(BF16), preferred_element_type=F32)


def _split_dot(a, b_bf16):
    hi = a.astype(BF16)
    lo = (a - hi.astype(F32)).astype(BF16)
    return (jnp.dot(hi, b_bf16, preferred_element_type=F32)
            + jnp.dot(lo, b_bf16, preferred_element_type=F32))


def _rms_rows(x):
    return x * lax.rsqrt(jnp.mean(x * x, axis=-1, keepdims=True) + NORM_EPS)


def _mod_kernel(c_ref, w_ref, b_ref, o_ref):
    c = c_ref[...]
    cond = c * jax.nn.sigmoid(c)
    o_ref[...] = _bdot(cond, w_ref[...]) + b_ref[...]


def _modulation(c, w_ada, b_ada):
    B, D = c.shape
    n_out = w_ada.shape[1]
    tn = D
    return pl.pallas_call(
        _mod_kernel,
        grid=(n_out // tn,),
        in_specs=[pl.BlockSpec((B, D), lambda j: (0, 0)),
                  pl.BlockSpec((D, tn), lambda j: (0, j)),
                  pl.BlockSpec((1, tn), lambda j: (0, j))],
        out_specs=pl.BlockSpec((B, tn), lambda j: (0, j)),
        out_shape=jax.ShapeDtypeStruct((B, n_out), F32),
        compiler_params=_cparams("arbitrary"),
        name="adaln_mod",
    )(c, w_ada, b_ada.reshape(1, n_out))


def _rope(x, cos, sin_signed):
    n = x.shape[-1]
    lane = lax.broadcasted_iota(jnp.int32, x.shape, 1)
    nxt = pltpu.roll(x, n - 1, axis=1)
    prv = pltpu.roll(x, 1, axis=1)
    swapped = jnp.where(lane % 2 == 0, nxt, prv)
    return x * cos + swapped * sin_signed


def _inproj_kernel(x_ref, xp_ref, xn_ref, sh_ref, sc_ref, gm_ref, w_ref, seg_ref, cos_ref,
                   sin_ref, qg_ref, kg_ref, mu_ref, wlo_ref, b0_ref, gup_ref,
                   q_out, k_out, v_out, r_out, kr_out, vr_out, wf_out, wb_out, af_out,
                   ab_out, g_out, *, att_w, kv_w, rw):
    i = pl.program_id(1)
    n_i = pl.num_programs(1)
    ts = x_ref.shape[1]
    gain = gm_ref[...]
    scale = 1.0 + sc_ref[0]
    shift = sh_ref[0]

    def norm_mod(xv):
        return (_rms_rows(xv) * gain * scale + shift).astype(BF16)

    halo = jnp.concatenate([xp_ref[0], xn_ref[0]], axis=0)
    h = jnp.concatenate([norm_mod(x_ref[0]), norm_mod(halo)], axis=0)
    p_all = jnp.dot(h, w_ref[...], preferred_element_type=F32)
    p = p_all[:ts]

    seg = seg_ref[...]

    def qk_norm_rope(z, gain_row, reps):
        ms = _split_dot(z * z, seg[: z.shape[1], : z.shape[1]])
        zn = z * lax.rsqrt(ms + NORM_EPS) * gain_row
        cos = jnp.concatenate([cos_ref[...]] * reps, axis=1) if reps > 1 else cos_ref[...]
        sin = jnp.concatenate([sin_ref[...]] * reps, axis=1) if reps > 1 else sin_ref[...]
        return _rope(zn, cos, sin)

    q = qk_norm_rope(p[:, :att_w], qg_ref[...], att_w // LANES)
    q_out[0] = (q * (HEAD_DIM ** -0.5 * LOG2_E)).astype(BF16)
    k = qk_norm_rope(p[:, att_w:att_w + kv_w], kg_ref[...], kv_w // LANES)
    k_out[0] = k.astype(BF16)
    v_out[0] = p[:, att_w + kv_w:att_w + 2 * kv_w].astype(BF16)

    off = att_w + 2 * kv_w
    pr = p[:, off:]
    ph = p_all[ts:, off:]
    prev_row = jnp.where(i == 0, 0.0, ph[SUBLANES - 1:SUBLANES, :])
    next_row = jnp.where(i == n_i - 1, 0.0, ph[SUBLANES:SUBLANES + 1, :])
    row = lax.broadcasted_iota(jnp.int32, pr.shape, 0)
    prev = jnp.where(row == 0, prev_row, pltpu.roll(pr, 1, axis=0))
    nxt = jnp.where(row == ts - 1, next_row, pltpu.roll(pr, ts - 1, axis=0))
    ps = pr + mu_ref[...] * (0.5 * (prev + nxt) - pr)

    r_out[0] = ps[:, :rw].T
    kr_out[0] = ps[:, rw:2 * rw].T
    vr_out[0] = ps[:, 2 * rw:3 * rw].T
    lo_in = ps[:, 3 * rw:3 * rw + DECAY_LORA + ICLR_LORA]
    lane = lax.broadcasted_iota(jnp.int32, lo_in.shape, 1)
    lo_in = jnp.where(lane < DECAY_LORA, jnp.tanh(lo_in), lo_in)
    lo = _bdot(lo_in, wlo_ref[...]) + b0_ref[...]
    wf_out[0] = jnp.exp(-DECAY_SCALE * jax.nn.sigmoid(lo[:, :rw])).T
    wb_out[0] = jnp.exp(-DECAY_SCALE * jax.nn.sigmoid(lo[:, rw:2 * rw])).T
    af_out[0] = jax.nn.sigmoid(lo[:, 2 * rw:3 * rw]).T
    ab_out[0] = jax.nn.sigmoid(lo[:, 3 * rw:]).T
    dg = ps[:, 3 * rw + DECAY_LORA + ICLR_LORA:]
    g_out[0] = _bdot(jax.nn.sigmoid(dg), gup_ref[...]).T


def _in_projection(x, sh1, sc1, g_mix, w_in, seg, cos_t, sin_t, q_norm, k_norm, mu_shift,
                   wlo, b0, g_up, *, ts):
    B, S, D = x.shape
    n_in = w_in.shape[1]
    rw = g_up.shape[1]
    kv_w = ATT_KV_HEADS * HEAD_DIM
    att_w = n_in - 2 * kv_w - (3 * rw + DECAY_LORA + ICLR_LORA + GATE_LORA)
    n_t = S // ts
    hb = ts // SUBLANES
    n_hb = S // SUBLANES
    const = lambda b, i: (0, 0)
    tile = lambda b, i: (b, i, 0)
    modspec = pl.BlockSpec((1, 1, D), lambda b, i: (b, 0, 0))
    in_specs = [
        pl.BlockSpec((1, ts, D), tile),
        pl.BlockSpec((1, SUBLANES, D), lambda b, i: (b, jnp.maximum(i * hb - 1, 0), 0)),
        pl.BlockSpec((1, SUBLANES, D), lambda b, i: (b, jnp.minimum((i + 1) * hb, n_hb - 1), 0)),
        modspec, modspec,
        pl.BlockSpec((1, D), const),
        pl.BlockSpec((D, n_in), const),
        pl.BlockSpec(seg.shape, const),
        pl.BlockSpec((ts, LANES), lambda b, i: (i, 0)),
        pl.BlockSpec((ts, LANES), lambda b, i: (i, 0)),
        pl.BlockSpec((1, att_w), const),
        pl.BlockSpec((1, kv_w), const),
        pl.BlockSpec((1, mu_shift.shape[1]), const),
        pl.BlockSpec(wlo.shape, const),
        pl.BlockSpec(b0.shape, const),
        pl.BlockSpec(g_up.shape, const),
    ]
    out_shapes = ([jax.ShapeDtypeStruct((B, S, att_w), BF16),
                   jax.ShapeDtypeStruct((B, S, kv_w), BF16),
                   jax.ShapeDtypeStruct((B, S, kv_w), BF16)]
                  + [jax.ShapeDtypeStruct((B, rw, S), F32)] * 8)
    out_specs = ([pl.BlockSpec((1, ts, att_w), tile),
                  pl.BlockSpec((1, ts, kv_w), tile),
                  pl.BlockSpec((1, ts, kv_w), tile)]
                 + [pl.BlockSpec((1, rw, ts), lambda b, i: (b, 0, i))] * 8)
    return pl.pallas_call(
        functools.partial(_inproj_kernel, att_w=att_w, kv_w=kv_w, rw=rw),
        grid=(B, n_t),
        in_specs=in_specs,
        out_specs=out_specs,
        out_shape=out_shapes,
        compiler_params=_cparams("parallel", "arbitrary"),
        name="in_projection",
    )(x, x, x, sh1, sc1, g_mix, w_in, seg, cos_t, sin_t, q_norm, k_norm, mu_shift, wlo, b0, g_up)


def _attn_kernel(q_ref, k_ref, v_ref, o_ref):
    n_kv = k_ref.shape[2] // HEAD_DIM
    grp = q_ref.shape[2] // (n_kv * HEAD_DIM)
    ones = jnp.ones((v_ref.shape[1], HEAD_DIM), BF16)
    outs = []
    for g in range(n_kv):
        k = k_ref[0, :, g * HEAD_DIM:(g + 1) * HEAD_DIM]
        v = jnp.concatenate([v_ref[0, :, g * HEAD_DIM:(g + 1) * HEAD_DIM], ones], axis=1)
        for hh in range(g * grp, (g + 1) * grp):
            q = q_ref[0, :, hh * HEAD_DIM:(hh + 1) * HEAD_DIM]
            s = lax.dot_general(q, k, (((1,), (1,)), ((), ())), preferred_element_type=F32)
            e = jnp.exp2(s - jnp.max(s, axis=-1, keepdims=True)).astype(BF16)
            o = jnp.dot(e, v, preferred_element_type=F32)
            outs.append(o[:, :HEAD_DIM] / o[:, HEAD_DIM:HEAD_DIM + 1])
    o_ref[0] = jnp.concatenate(outs, axis=1).astype(o_ref.dtype)


def _attention(q, k, v, *, tq):
    B, S, att_w = q.shape
    kv_w = k.shape[2]
    tile = pl.BlockSpec((1, tq, att_w), lambda b, i: (b, i, 0))
    keys = pl.BlockSpec((1, S, kv_w), lambda b, i: (b, 0, 0))
    return pl.pallas_call(
        _attn_kernel,
        grid=(B, S // tq),
        in_specs=[tile, keys, keys],
        out_specs=tile,
        out_shape=jax.ShapeDtypeStruct((B, S, att_w), BF16),
        compiler_params=_cparams("parallel", "arbitrary"),
        name="attention",
    )(q, k, v)


def _wkv_kernel(*refs, reverse, merge):
    r_ref, k_ref, v_ref, w_ref, a_ref, kk_scale_ref, ka_ref, rk_ref = refs[:8]
    other = refs[8:12] if merge else ()
    y_ref, bs_ref, s_ref, g_ref, ca_ref, cb_ref, cr_ref, cn_ref = refs[8 + len(other):]
    c = pl.program_id(1)
    tc, n, _ = r_ref.shape

    @pl.when(c == 0)
    def _():
        s_ref[...] = jnp.zeros_like(s_ref)

    def tidx(j):
        return (tc - 1 - j) if reverse else j

    def prepare(j, g):
        t = tidx(j)
        kt = k_ref[t]
        a = a_ref[t]
        r = r_ref[t]
        kk = kt * kk_scale_ref[...]
        kk = kk / jnp.maximum(jnp.sqrt(jnp.sum(kk * kk, axis=0, keepdims=True)), L2_EPS)
        kd = kt * (1.0 + (a - 1.0) * ka_ref[...])
        bs_ref[t] = jnp.sum(r * kd * rk_ref[...], axis=0, keepdims=True)
        cn_ref[t] = -(kk * g)
        g = g * w_ref[t]
        g_inv = 1.0 / g
        ca_ref[t] = kk * a * g_inv
        cb_ref[t] = kd * g_inv
        cr_ref[t] = r * g
        return g

    g_ref[...] = lax.fori_loop(0, tc, prepare, jnp.ones((n, LANES), F32), unroll=4)

    t0 = tidx(0)
    sa0 = jnp.zeros((n, LANES), F32)
    for kx in range(n):
        sa0 = sa0 + s_ref[kx] * cn_ref[t0, kx:kx + 1, :]

    def step(j, sa):
        t = tidx(j)
        tn = tidx(jnp.minimum(j + 1, tc - 1))
        v_t = v_ref[t]
        y = jnp.zeros((n, LANES), F32)
        sa_next = jnp.zeros((n, LANES), F32)
        for kx in range(n):
            row = pl.ds(kx, 1)
            z_new = s_ref[kx] + sa * ca_ref[t, row, :] + v_t * cb_ref[t, row, :]
            s_ref[kx] = z_new
            y = y + z_new * cr_ref[t, row, :]
            sa_next = sa_next + z_new * cn_ref[tn, row, :]
        y_ref[t] = y
        return sa_next

    lax.fori_loop(0, tc, step, sa0)

    for kx in range(n):
        s_ref[kx] = s_ref[kx] * g_ref[kx:kx + 1, :]

    if merge:
        yo_ref, bo_ref, lw_ref, lb_ref = other

        def finish(t, carry):
            y = y_ref[t] + yo_ref[t]
            d = y - jnp.mean(y, axis=0, keepdims=True)
            var = jnp.mean(d * d, axis=0, keepdims=True)
            y_ref[t] = (d * lax.rsqrt(var + GN_EPS) * lw_ref[...] + lb_ref[...]
                        + (bs_ref[t] + bo_ref[t]) * v_ref[t])
            return carry

        lax.fori_loop(0, tc, finish, 0, unroll=4)


def _wkv_scan(r, k, v, w, a, kk_scale, ka, rk, other=None, *, reverse, tc):
    S, N, BH = r.shape
    n_c = S // tc
    n_l = BH // LANES
    if reverse:
        tmap = lambda j, c: (n_c - 1 - c, 0, j)
    else:
        tmap = lambda j, c: (c, 0, j)
    blk = pl.BlockSpec((tc, N, LANES), tmap)
    sblk = pl.BlockSpec((tc, 1, LANES), tmap)
    par = pl.BlockSpec((N, LANES), lambda j, c: (0, j))
    merge = other is not None
    extra_specs = [blk, sblk, par, par] if merge else []
    return pl.pallas_call(
        functools.partial(_wkv_kernel, reverse=reverse, merge=merge),
        grid=(n_l, n_c),
        in_specs=[blk] * 5 + [par] * 3 + extra_specs,
        out_specs=[blk, sblk],
        out_shape=[jax.ShapeDtypeStruct((S, N, BH), F32),
                   jax.ShapeDtypeStruct((S, 1, BH), F32)],
        scratch_shapes=([pltpu.VMEM((N, N, LANES), F32), pltpu.VMEM((N, LANES), F32)]
                        + [pltpu.VMEM((tc, N, LANES), F32)] * 4),
        compiler_params=_cparams("parallel", "arbitrary"),
        name="wkv7_scan_bwd" if reverse else "wkv7_scan_fwd",
    )(r, k, v, w, a, kk_scale, ka, rk, *(other or ()))


def _outproj_kernel(oa_ref, yo_ref, g_ref, x_ref, gt_ref, wo_ref, gf_ref, sh_ref, sc_ref,
                    wr_ref, x1_out, h2_out, aff_out):
    yg = (yo_ref[0] * g_ref[0]).T.astype(BF16)
    mix = jnp.concatenate([oa_ref[0], yg], axis=1)
    x1 = x_ref[0] + gt_ref[0] * jnp.dot(mix, wo_ref[...], preferred_element_type=F32)
    x1_out[0] = x1
    h2 = _rms_rows(x1) * gf_ref[...] * (1.0 + sc_ref[0]) + sh_ref[0]
    h2_out[0] = h2.astype(BF16)
    wr = wr_ref[...]
    nt = (((1,), (1,)), ((), ()))
    w_hi = wr.astype(BF16)
    w_lo = (wr - w_hi.astype(F32)).astype(BF16)
    h_hi = h2.astype(BF16)
    h_lo = (h2 - h_hi.astype(F32)).astype(BF16)
    logits = (lax.dot_general(w_hi, h_hi, nt, preferred_element_type=F32)
              + lax.dot_general(w_hi, h_lo, nt, preferred_element_type=F32)
              + lax.dot_general(w_lo, h_hi, nt, preferred_element_type=F32))
    m = jnp.max(logits, axis=0, keepdims=True)
    e = jnp.exp(logits - m)
    aff_out[0] = e / jnp.sum(e, axis=0, keepdims=True)


def _out_projection(o_att, yo, g, x, gt1, w_out, g_ffn, sh2, sc2, w_router_t, *, ts):
    B, S, D = x.shape
    aw = o_att.shape[2]
    rw = yo.shape[1]
    E = w_router_t.shape[0]
    tile = lambda b, i: (b, i, 0)
    const = lambda b, i: (0, 0)
    modspec = pl.BlockSpec((1, 1, D), lambda b, i: (b, 0, 0))
    cmajor = pl.BlockSpec((1, rw, ts), lambda b, i: (b, 0, i))
    return pl.pallas_call(
        _outproj_kernel,
        grid=(B, S // ts),
        in_specs=[pl.BlockSpec((1, ts, aw), tile),
                  cmajor,
                  cmajor,
                  pl.BlockSpec((1, ts, D), tile),
                  modspec,
                  pl.BlockSpec((aw + rw, D), const),
                  pl.BlockSpec((1, D), const),
                  modspec, modspec,
                  pl.BlockSpec((E, D), const)],
        out_specs=[pl.BlockSpec((1, ts, D), tile),
                   pl.BlockSpec((1, ts, D), tile),
                   pl.BlockSpec((1, E, ts), lambda b, i: (b, 0, i))],
        out_shape=[jax.ShapeDtypeStruct((B, S, D), F32),
                   jax.ShapeDtypeStruct((B, S, D), BF16),
                   jax.ShapeDtypeStruct((B, E, S), F32)],
        compiler_params=_cparams("parallel", "arbitrary"),
        name="out_projection",
    )(o_att, yo, g, x, gt1, w_out, g_ffn, sh2, sc2, w_router_t)


def _route_kernel(aff_ref, tri_ref, wgt_out, pos_out, *, cap):
    aff = aff_ref[...]
    capf = jnp.float32(cap)

    def count(mask):
        return jnp.sum(jnp.where(mask, 1.0, 0.0), axis=1, keepdims=True)

    def as_float(word):
        return pltpu.bitcast(word, F32)

    def bit_step(i, thr):
        cand = thr | jnp.left_shift(jnp.int32(1), 30 - i)
        return jnp.where(count(aff >= as_float(cand)) >= capf, cand, thr)

    thr = lax.fori_loop(0, 31, bit_step, jnp.zeros((aff.shape[0], 1), jnp.int32))
    thr_f = as_float(thr)
    gt = aff > thr_f
    eq = aff == thr_f
    need = capf - count(gt)
    tri = tri_ref[...]
    eq_rank = jnp.dot(jnp.where(eq, 1.0, 0.0).astype(BF16), tri, preferred_element_type=F32)
    sel = jnp.where(gt, 1.0, jnp.where(eq, jnp.where(eq_rank <= need, 1.0, 0.0), 0.0))
    rank = jnp.dot(sel.astype(BF16), tri, preferred_element_type=F32)
    chosen = sel > 0.5
    wgt_out[...] = jnp.where(chosen, aff, 0.0)
    pos_out[...] = jnp.where(chosen, rank - 1.0, -1.0).astype(jnp.int32)


def _routing(aff, tri, *, cap):
    B, E, S = aff.shape
    rows = _pick_tile(B * E, LANES)
    blk = pl.BlockSpec((rows, S), lambda i: (i, 0))
    wgt, pos = pl.pallas_call(
        functools.partial(_route_kernel, cap=cap),
        grid=(B * E // rows,),
        in_specs=[blk, pl.BlockSpec((S, S), lambda i: (0, 0))],
        out_specs=[blk, blk],
        out_shape=[jax.ShapeDtypeStruct((B * E, S), F32),
                   jax.ShapeDtypeStruct((B * E, S), jnp.int32)],
        compiler_params=_cparams("parallel"),
        name="ec_routing",
    )(aff.reshape(B * E, S), tri)
    return wgt.reshape(B, E, S), pos.reshape(B, E, S)


def _expert_kernel(h_ref, pos_ref, wgt_ref, wg_ref, wu_ref, wd_ref, x1_ref, gt_ref, acc_ref,
                   *, cap):
    e = pl.program_id(1)

    @pl.when(e == 0)
    def _():
        acc_ref[...] = jnp.zeros_like(acc_ref)

    slab = x1_ref.shape[1]
    rows = pl.ds(pl.multiple_of(e * slab, slab), slab)
    acc_ref[0, rows, :] += x1_ref[0]

    pos = pos_ref[0, 0]
    wgt = wgt_ref[0, 0]
    s = pos.shape[1]
    slot = lax.broadcasted_iota(jnp.int32, (cap, s), 0)
    hit = pos == slot
    onehot = jnp.where(hit, 1.0, 0.0).astype(BF16)
    vals = jnp.sum(jnp.where(hit, wgt, 0.0), axis=1, keepdims=True)
    hg = jnp.dot(onehot, h_ref[0], preferred_element_type=F32).astype(BF16)
    gate = jnp.dot(hg, wg_ref[0], preferred_element_type=F32)
    up = jnp.dot(hg, wu_ref[0], preferred_element_type=F32)
    hid = (gate * jax.nn.sigmoid(gate) * up).astype(BF16)
    y = jnp.dot(hid, wd_ref[0], preferred_element_type=F32) * vals * gt_ref[0]
    acc_ref[0] += lax.dot_general(onehot, y.astype(BF16), (((0,), (0,)), ((), ())),
                                  preferred_element_type=F32)


def _experts(h2, pos, wgt, w_gate, w_up, w_down, x1, gt2, *, cap):
    B, S, D = h2.shape
    E, _, F = w_gate.shape
    row = pl.BlockSpec((1, 1, 1, S), lambda b, e: (b, e, 0, 0))
    return pl.pallas_call(
        functools.partial(_expert_kernel, cap=cap),
        grid=(B, E),
        in_specs=[pl.BlockSpec((1, S, D), lambda b, e: (b, 0, 0)),
                  row, row,
                  pl.BlockSpec((1, D, F), lambda b, e: (e, 0, 0)),
                  pl.BlockSpec((1, D, F), lambda b, e: (e, 0, 0)),
                  pl.BlockSpec((1, F, D), lambda b, e: (e, 0, 0)),
                  pl.BlockSpec((1, S // E, D), lambda b, e: (b, e, 0)),
                  pl.BlockSpec((1, 1, D), lambda b, e: (b, 0, 0))],
        out_specs=pl.BlockSpec((1, S, D), lambda b, e: (b, 0, 0)),
        out_shape=jax.ShapeDtypeStruct((B, S, D), F32),
        compiler_params=_cparams("parallel", "arbitrary"),
        name="ec_experts",
    )(h2, pos.reshape(B, E, 1, S), wgt.reshape(B, E, 1, S), w_gate, w_up, w_down, x1, gt2)


def _rope_tables(seq):
    rows = seq // GRID_W
    row = jnp.repeat(jnp.arange(rows, dtype=F32), GRID_W)
    col = jnp.tile(jnp.arange(GRID_W, dtype=F32), rows)
    n_pairs_axis = HEAD_DIM // 4
    freqs = ROPE_THETA ** (-jnp.arange(n_pairs_axis, dtype=F32) / n_pairs_axis)
    ang = jnp.concatenate([row[:, None] * freqs, col[:, None] * freqs], axis=-1)
    cos = jnp.repeat(jnp.cos(ang), 2, axis=1)
    sin = jnp.repeat(jnp.sin(ang), 2, axis=1) * jnp.tile(jnp.array([-1.0, 1.0], F32), HEAD_DIM // 2)
    reps = LANES // HEAD_DIM
    return jnp.tile(cos, (1, reps)), jnp.tile(sin, (1, reps))


def _to_scan_layout(z, bh_pad):
    B, R, S = z.shape
    H = R // RWKV_HEAD
    zt = z.reshape(B, H, RWKV_HEAD, S).transpose(3, 2, 0, 1).reshape(S, RWKV_HEAD, B * H)
    if bh_pad != B * H:
        zt = jnp.pad(zt, ((0, 0), (0, 0), (0, bh_pad - B * H)))
    return zt


def _lane_param(p_hn, B, bh_pad):
    H, N = p_hn.shape
    t = jnp.tile(p_hn.T[:, None, :], (1, B, 1)).reshape(N, B * H)
    if bh_pad != B * H:
        t = jnp.pad(t, ((0, 0), (0, bh_pad - B * H)))
    return t


def _pick_tile(n, pref):
    t = min(n, pref)
    while n % t:
        t //= 2
    return t


def kernel(x, c, w_ada, b_ada, g_mix, w_in, q_norm, k_norm, mu_shift, w0, w_up, a0, a_up, g_up,
           k_k, k_a, r_k, ln_w, ln_b, w_out, g_ffn, w_router, w_gate, w_up_e, w_down):
    B, S, D = x.shape
    depth = w_ada.shape[0]
    rw = g_up.shape[2]
    H = rw // RWKV_HEAD
    cap = EC_CAPACITY * S // N_EXPERTS
    ts = _pick_tile(S, 512)
    tq = _pick_tile(S, 1024)
    tc = _pick_tile(S, 64)
    bh = B * H
    bh_pad = -(-bh // LANES) * LANES

    cos_t, sin_t = _rope_tables(S)
    seg_n = max(w_in.shape[2] - 2 * ATT_KV_HEADS * HEAD_DIM
                - (3 * rw + DECAY_LORA + ICLR_LORA + GATE_LORA), ATT_KV_HEADS * HEAD_DIM)
    seg_id = np.arange(seg_n) // HEAD_DIM
    seg = jnp.asarray((seg_id[:, None] == seg_id[None, :]).astype(np.float32) / HEAD_DIM, BF16)
    tri = jnp.asarray(np.arange(S)[:, None] <= np.arange(S)[None, :], BF16)

    for l in range(depth):
        mod = _modulation(c, w_ada[l].astype(BF16), b_ada[l])
        sh1, sc1, gt1, sh2, sc2, gt2 = [m.reshape(B, 1, D) for m in jnp.split(mod, 6, axis=-1)]

        zeros = jnp.zeros((DECAY_LORA, rw), F32)
        wlo = jnp.concatenate([
            jnp.concatenate([w_up[l, 0], w_up[l, 1], zeros, zeros], axis=1),
            jnp.concatenate([zeros, zeros, a_up[l, 0], a_up[l, 1]], axis=1)], axis=0).astype(BF16)
        b0 = jnp.concatenate([w0[l, 0], w0[l, 1], a0[l, 0], a0[l, 1]]).reshape(1, 4 * rw)
        att_w = seg_n
        qg = jnp.tile(q_norm[l], att_w // HEAD_DIM).reshape(1, att_w)
        kg = jnp.tile(k_norm[l], ATT_KV_HEADS).reshape(1, ATT_KV_HEADS * HEAD_DIM)

        (q, k, v, r, kr, vr, wf, wb, af, ab, g) = _in_projection(
            x, sh1, sc1, g_mix[l].reshape(1, D), w_in[l].astype(BF16), seg, cos_t, sin_t, qg, kg,
            mu_shift[l].reshape(1, -1), wlo, b0, g_up[l].astype(BF16), ts=ts)

        o_att = _attention(q, k, v, tq=tq)

        rt, kt, vt, wft, wbt, aft, abt = [_to_scan_layout(z, bh_pad)
                                          for z in (r, kr, vr, wf, wb, af, ab)]
        kk_scale = _lane_param(k_k[l].reshape(H, RWKV_HEAD), B, bh_pad)
        ka = _lane_param(k_a[l].reshape(H, RWKV_HEAD), B, bh_pad)
        rk = _lane_param(r_k[l], B, bh_pad)
        lw = _lane_param(ln_w[l].reshape(H, RWKV_HEAD), B, bh_pad)
        lb = _lane_param(ln_b[l].reshape(H, RWKV_HEAD), B, bh_pad)
        yf, bf = _wkv_scan(rt, kt, vt, wft, aft, kk_scale, ka, rk, reverse=False, tc=tc)
        yo_t, _ = _wkv_scan(rt, kt, vt, wbt, abt, kk_scale, ka, rk, (yf, bf, lw, lb),
                            reverse=True, tc=tc)
        yo = (yo_t[:, :, :bh].reshape(S, RWKV_HEAD, B, H).transpose(2, 3, 1, 0)
              .reshape(B, rw, S))

        x1, h2, aff = _out_projection(o_att, yo, g, x, gt1, w_out[l].astype(BF16),
                                      g_ffn[l].reshape(1, D), sh2, sc2, w_router[l].T,
                                      ts=_pick_tile(S, 1024))
        wgt, pos = _routing(aff, tri, cap=cap)
        x = _experts(h2, pos, wgt, w_gate[l].astype(BF16), w_up_e[l].astype(BF16),
                     w_down[l].astype(BF16), x1, gt2, cap=cap)
    return x
```
